```python
import math
import jax
import jax.numpy as jnp
from jax import lax
import numpy as np

D_MODEL = 2048
BATCH = 2
SEQ = 8192
DEPTH = 2
DEC_BATCH = 16
DEC_SEQ = 16
PAST_LEN = 4096

CHUNK = 64
N_META = 16
MIX_WIDTH = D_MODEL
LRU_WIDTH = MIX_WIDTH // 2
LRU_BLOCKS = 16
LRU_BLOCK = LRU_WIDTH // LRU_BLOCKS
LRU_C = 8.0
CONV_W = 4
GDN_HEADS = 8
GDN_DK = 128
GDN_DV = (MIX_WIDTH - LRU_WIDTH) // GDN_HEADS
GDN_QK = GDN_HEADS * GDN_DK
GDN_VW = GDN_HEADS * GDN_DV
GDN_QKV = 2 * GDN_QK + GDN_VW
N_IN = 2 * LRU_WIDTH + GDN_QKV + GDN_VW + 2 * GDN_HEADS
D_FF = 5632
EPS = 1e-6

kernel_name = 'hymba_rglru_gdn_macaron_stream_step'


def rms_norm(x, w):
    xf = x.astype(jnp.float32)
    y = xf * lax.rsqrt(jnp.mean(xf * xf, axis=-1, keepdims=True) + EPS)
    return (y * w.astype(jnp.float32)).astype(x.dtype)


def l2_normalize(x):
    return x * lax.rsqrt(jnp.sum(x * x, axis=-1, keepdims=True) + EPS)


def swiglu(h, w_gate, w_up, w_down):
    return (jax.nn.silu(h @ w_gate) * (h @ w_up)) @ w_down


def causal_conv(x, hist, w, b):
    t = x.shape[1]
    xp = jnp.concatenate([hist.astype(x.dtype), x], axis=1)
    y = xp[:, 0:t] * w[0]
    for i in range(1, CONV_W):
        y = y + xp[:, i:i + t] * w[i]
    if b is not None:
        y = y + b
    return y, xp[:, xp.shape[1] - (CONV_W - 1):]


def _lin_combine(earlier, later):
    a1, b1 = earlier
    a2, b2 = later
    return a1 * a2, a2 * b1 + b2


def rg_lru(x, rg_w, rg_b, ig_w, ig_b, lam, h0):
    bsz, t, _ = x.shape
    xf = x.astype(jnp.float32)
    xb = xf.reshape(bsz, t, LRU_BLOCKS, LRU_BLOCK)
    r = jax.nn.sigmoid(jnp.einsum('btni,nij->btnj', xb, rg_w.astype(jnp.float32)).reshape(bsz, t, LRU_WIDTH) + rg_b.astype(jnp.float32))
    ig = jax.nn.sigmoid(jnp.einsum('btni,nij->btnj', xb, ig_w.astype(jnp.float32)).reshape(bsz, t, LRU_WIDTH) + ig_b.astype(jnp.float32))
    log_a = -LRU_C * r * jax.nn.softplus(-lam.astype(jnp.float32))
    a = jnp.exp(log_a)
    b = jnp.sqrt(-jnp.expm1(2.0 * log_a)) * ig * xf
    a_cum, b_cum = lax.associative_scan(_lin_combine, (a, b), axis=1)
    h = a_cum * h0.astype(jnp.float32)[:, None, :] + b_cum
    return h, h[:, -1]


def _to_blocks(a, pad, n):
    a = jnp.pad(a, [(0, 0), (pad, 0)] + [(0, 0)] * (a.ndim - 2))
    a = a.reshape((a.shape[0], n, CHUNK) + a.shape[2:])
    return jnp.moveaxis(a, (1, 3), (0, 2))


def gated_delta_rule(q, k, v, g, beta, s0):
    bsz, t, h, _ = q.shape
    dv = v.shape[-1]
    pad = (-t) % CHUNK
    n = (t + pad) // CHUNK
    qb, kb, vb = _to_blocks(q, pad, n), _to_blocks(k, pad, n), _to_blocks(v, pad, n)
    gb, bb = _to_blocks(g, pad, n), _to_blocks(beta, pad, n)
    gc = jnp.cumsum(gb, axis=-1)
    idx = jnp.arange(CHUNK)
    incl = idx[:, None] >= idx[None, :]
    strict = idx[:, None] > idx[None, :]
    diff = gc[..., :, None] - gc[..., None, :]
    decay = jnp.where(incl, jnp.exp(jnp.where(incl, diff, 0.0)), 0.0)
    kk = jnp.einsum('nbhid,nbhjd->nbhij', kb, kb)
    lmat = jnp.where(strict, bb[..., :, None] * decay * kk, 0.0) + jnp.eye(CHUNK, dtype=jnp.float32)
    rhs = jnp.concatenate([bb[..., None] * vb, (bb * jnp.exp(gc))[..., None] * kb], axis=-1)
    sol = lax.linalg.triangular_solve(lmat, rhs, left_side=True, lower=True, unit_diagonal=True)
    u, wk = sol[..., :dv], sol[..., dv:]
    qk = jnp.einsum('nbhid,nbhjd->nbhij', qb, kb) * decay
    q_dec = qb * jnp.exp(gc)[..., None]
    k_end = kb * jnp.exp(gc[..., -1:] - gc)[..., None]
    g_end = jnp.exp(gc[..., -1])

    def step(s, xs):
        u_c, wk_c, qk_c, qd_c, ke_c, ge_c = xs
        w = u_c - jnp.einsum('bhik,bhkv->bhiv', wk_c, s)
        o = jnp.einsum('bhik,bhkv->bhiv', qd_c, s) + jnp.einsum('bhij,bhjv->bhiv', qk_c, w)
        s = ge_c[..., None, None] * s + jnp.einsum('bhjk,bhjv->bhkv', ke_c, w)
        return s, o

    s_last, o = lax.scan(step, s0.astype(jnp.float32), (u, wk, qk, q_dec, k_end, g_end))
    o = jnp.moveaxis(o, (0, 2), (1, 3)).reshape(bsz, n * CHUNK, h, dv)[:, pad:]
    return o, s_last


def mixer(h, conv_a_hist, lru_h0, conv_b_hist, s0, p, l):
    bsz, t, _ = h.shape
    proj = h @ p['w_in'][l]
    o1 = LRU_WIDTH
    o2 = 2 * LRU_WIDTH
    o3 = o2 + GDN_QKV
    o4 = o3 + GDN_VW
    o5 = o4 + GDN_HEADS
    xa, ga, qkv = proj[..., :o1], proj[..., o1:o2], proj[..., o2:o3]
    z, b_in, al_in = proj[..., o3:o4], proj[..., o4:o5], proj[..., o5:]
    xa_c, new_conv_a = causal_conv(xa, conv_a_hist, p['conv_a_w'][l], p['conv_a_b'][l])
    ha, lru_last = rg_lru(xa_c, p['rg_w'][l], p['rg_b'][l], p['ig_w'][l], p['ig_b'][l], p['lru_lambda'][l], lru_h0)
    ya = rms_norm(ha, p['norm_a'][l]) * jax.nn.gelu(ga.astype(jnp.float32))
    qkv_c, new_conv_b = causal_conv(qkv, conv_b_hist, p['conv_b_w'][l], None)
    qkv_c = jax.nn.silu(qkv_c.astype(jnp.float32))
    q = qkv_c[..., :GDN_QK].reshape(bsz, t, GDN_HEADS, GDN_DK)
    k = qkv_c[..., GDN_QK:2 * GDN_QK].reshape(bsz, t, GDN_HEADS, GDN_DK)
    v = qkv_c[..., 2 * GDN_QK:].reshape(bsz, t, GDN_HEADS, GDN_DV)
    q = l2_normalize(q) * (GDN_DK ** -0.5)
    k = l2_normalize(k)
    beta = jax.nn.sigmoid(b_in.astype(jnp.float32))
    g = -jnp.exp(p['a_log'][l].astype(jnp.float32)) * jax.nn.softplus(al_in.astype(jnp.float32) + p['dt_bias'][l].astype(jnp.float32))
    o, s_last = gated_delta_rule(q, k, v, g, beta, s0)
    yb = rms_norm(o, p['norm_b'][l]) * jax.nn.silu(z.astype(jnp.float32).reshape(bsz, t, GDN_HEADS, GDN_DV))
    y = jnp.concatenate([ya, yb.reshape(bsz, t, GDN_VW)], axis=-1).astype(h.dtype) @ p['w_out'][l]
    return y, new_conv_a, lru_last, new_conv_b, s_last


def trunk(x, conv_a_st, lru_st, conv_b_st, delta_st, p):
    ca, lr, cb, dl = [], [], [], []
    for l in range(DEPTH):
        x = x + 0.5 * swiglu(rms_norm(x, p['ffn1_norm'][l]), p['ffn1_w_gate'][l], p['ffn1_w_up'][l], p['ffn1_w_down'][l])
        y, nca, nlr, ncb, ndl = mixer(rms_norm(x, p['mix_norm'][l]), conv_a_st[l], lru_st[l], conv_b_st[l], delta_st[l], p, l)
        x = x + y
        x = x + 0.5 * swiglu(rms_norm(x, p['ffn2_norm'][l]), p['ffn2_w_gate'][l], p['ffn2_w_up'][l], p['ffn2_w_down'][l])
        ca.append(nca)
        lr.append(nlr)
        cb.append(ncb)
        dl.append(ndl)
    y = rms_norm(x, p['final_norm'])
    return y, jnp.stack(ca, 0), jnp.stack(lr, 0), jnp.stack(cb, 0), jnp.stack(dl, 0)


def setup_inputs(seed: int = 0) -> dict:
    key = jax.random.key(seed)
    ks = iter(jax.random.split(key, 48))
    f32 = jnp.float32

    def nrm(shape, scale):
        return jax.random.normal(next(ks), shape, f32) * scale

    def gain(shape):
        return 1.0 + nrm(shape, 0.02)

    x_prompt = nrm((BATCH, SEQ, D_MODEL), 1.0)
    x_sample = nrm((DEC_BATCH, DEC_SEQ, D_MODEL), 1.0)
    state_conv_a = nrm((DEPTH, DEC_BATCH, CONV_W - 1, LRU_WIDTH), 1.0)
    state_lru = nrm((DEPTH, DEC_BATCH, LRU_WIDTH), 0.5)
    state_conv_b = nrm((DEPTH, DEC_BATCH, CONV_W - 1, GDN_QKV), 1.0)
    state_delta = nrm((DEPTH, DEC_BATCH, GDN_HEADS, GDN_DK, GDN_DV), 0.05)
    meta_tokens = nrm((N_META, D_MODEL), 1.0)
    ffn1_norm = gain((DEPTH, D_MODEL))
    ffn1_w_gate = nrm((DEPTH, D_MODEL, D_FF), D_MODEL ** -0.5)
    ffn1_w_up = nrm((DEPTH, D_MODEL, D_FF), D_MODEL ** -0.5)
    ffn1_w_down = nrm((DEPTH, D_FF, D_MODEL), D_FF ** -0.5)
    mix_norm = gain((DEPTH, D_MODEL))
    w_in = nrm((DEPTH, D_MODEL, N_IN), D_MODEL ** -0.5)
    conv_a_w = nrm((DEPTH, CONV_W, LRU_WIDTH), CONV_W ** -0.5)
    conv_a_b = nrm((DEPTH, LRU_WIDTH), 0.01)
    rg_w = nrm((DEPTH, LRU_BLOCKS, LRU_BLOCK, LRU_BLOCK), LRU_BLOCK ** -0.5)
    rg_b = nrm((DEPTH, LRU_WIDTH), 0.01)
    ig_w = nrm((DEPTH, LRU_BLOCKS, LRU_BLOCK, LRU_BLOCK), LRU_BLOCK ** -0.5)
    ig_b = nrm((DEPTH, LRU_WIDTH), 0.01)
    a_pow = jax.random.uniform(next(ks), (DEPTH, LRU_WIDTH), f32, 0.9, 0.999) ** (1.0 / LRU_C)
    lru_lambda = jnp.log(a_pow) - jnp.log1p(-a_pow)
    norm_a = gain((DEPTH, LRU_WIDTH))
    conv_b_w = nrm((DEPTH, CONV_W, GDN_QKV), CONV_W ** -0.5)
    a_log = jnp.log(jax.random.uniform(next(ks), (DEPTH, GDN_HEADS), f32, 1.0, 16.0))
    dt = jnp.exp(jax.random.uniform(next(ks), (DEPTH, GDN_HEADS), f32, math.log(0.001), math.log(0.1)))
    dt_bias = dt + jnp.log(-jnp.expm1(-dt))
    norm_b = gain((DEPTH, GDN_DV))
    w_out = nrm((DEPTH, MIX_WIDTH, D_MODEL), MIX_WIDTH ** -0.5)
    ffn2_norm = gain((DEPTH, D_MODEL))
    ffn2_w_gate = nrm((DEPTH, D_MODEL, D_FF), D_MODEL ** -0.5)
    ffn2_w_up = nrm((DEPTH, D_MODEL, D_FF), D_MODEL ** -0.5)
    ffn2_w_down = nrm((DEPTH, D_FF, D_MODEL), D_FF ** -0.5)
    final_norm = gain((D_MODEL,))
    return {'x_prompt': x_prompt, 'x_sample': x_sample,
            'state_conv_a': state_conv_a, 'state_lru': state_lru,
            'state_conv_b': state_conv_b, 'state_delta': state_delta,
            'meta_tokens': meta_tokens,
            'ffn1_norm': ffn1_norm, 'ffn1_w_gate': ffn1_w_gate, 'ffn1_w_up': ffn1_w_up, 'ffn1_w_down': ffn1_w_down,
            'mix_norm': mix_norm, 'w_in': w_in,
            'conv_a_w': conv_a_w, 'conv_a_b': conv_a_b, 'rg_w': rg_w, 'rg_b': rg_b, 'ig_w': ig_w, 'ig_b': ig_b,
            'lru_lambda': lru_lambda, 'norm_a': norm_a,
            'conv_b_w': conv_b_w, 'a_log': a_log, 'dt_bias': dt_bias, 'norm_b': norm_b,
            'w_out': w_out,
            'ffn2_norm': ffn2_norm, 'ffn2_w_gate': ffn2_w_gate, 'ffn2_w_up': ffn2_w_up, 'ffn2_w_down': ffn2_w_down,
            'final_norm': final_norm}


def reference(x_prompt, x_sample, state_conv_a, state_lru, state_conv_b, state_delta, meta_tokens,
              ffn1_norm, ffn1_w_gate, ffn1_w_up, ffn1_w_down, mix_norm, w_in,
              conv_a_w, conv_a_b, rg_w, rg_b, ig_w, ig_b, lru_lambda, norm_a,
              conv_b_w, a_log, dt_bias, norm_b, w_out,
              ffn2_norm, ffn2_w_gate, ffn2_w_up, ffn2_w_down, final_norm):
    p = {'ffn1_norm': ffn1_norm, 'ffn1_w_gate': ffn1_w_gate, 'ffn1_w_up': ffn1_w_up, 'ffn1_w_down': ffn1_w_down,
         'mix_norm': mix_norm, 'w_in': w_in,
         'conv_a_w': conv_a_w, 'conv_a_b': conv_a_b, 'rg_w': rg_w, 'rg_b': rg_b, 'ig_w': ig_w, 'ig_b': ig_b,
         'lru_lambda': lru_lambda, 'norm_a': norm_a,
         'conv_b_w': conv_b_w, 'a_log': a_log, 'dt_bias': dt_bias, 'norm_b': norm_b, 'w_out': w_out,
         'ffn2_norm': ffn2_norm, 'ffn2_w_gate': ffn2_w_gate, 'ffn2_w_up': ffn2_w_up, 'ffn2_w_down': ffn2_w_down,
         'final_norm': final_norm}
    bp = x_prompt.shape[0]
    meta = jnp.broadcast_to(meta_tokens.astype(x_prompt.dtype)[None], (bp, N_META, D_MODEL))
    xp = jnp.concatenate([meta, x_prompt], axis=1)
    z_ca = jnp.zeros((DEPTH, bp, CONV_W - 1, LRU_WIDTH), x_prompt.dtype)
    z_lru = jnp.zeros((DEPTH, bp, LRU_WIDTH), jnp.float32)
    z_cb = jnp.zeros((DEPTH, bp, CONV_W - 1, GDN_QKV), x_prompt.dtype)
    z_dl = jnp.zeros((DEPTH, bp, GDN_HEADS, GDN_DK, GDN_DV), jnp.float32)
    yp, p_ca, p_lru, p_cb, p_dl = trunk(xp, z_ca, z_lru, z_cb, z_dl, p)
    ys, s_ca, s_lru, s_cb, s_dl = trunk(x_sample, state_conv_a, state_lru, state_conv_b, state_delta, p)
    return (yp[:, N_META:], ys, p_ca, p_lru, p_cb, p_dl, s_ca, s_lru, s_cb, s_dl)
```

```python
import functools

import jax
import jax.numpy as jnp
from jax import lax
from jax.experimental import pallas as pl
from jax.experimental.pallas import tpu as pltpu

F32 = jnp.float32
BF16 = jnp.bfloat16

D_MODEL = 2048
BATCH = 2
SEQ = 8192
DEPTH = 2
DEC_BATCH = 16
DEC_SEQ = 16
N_META = 16
LRU_WIDTH = 1024
LRU_BLOCKS = 16
LRU_BLOCK = 64
LRU_C = 8.0
CONV_W = 4
GDN_HEADS = 8
GDN_DK = 128
GDN_DV = 128
GDN_QK = 1024
GDN_VW = 1024
GDN_QKV = 3072
N_IN = 6160
D_FF = 5632
EPS = 1e-6

LANES = 128
SUBLANES = 8
CHUNK = 128
VALID_P = N_META + SEQ
TP = 8320
ROWS_P = BATCH * TP
ROWS_S = DEC_BATCH * DEC_SEQ
ROWS = ROWS_P + ROWS_S
N_IN_PAD = 6272
COL_XA, COL_GA, COL_Q, COL_K, COL_V, COL_Z = 0, 1, 2, 3, 4, 5
COL_TAIL = 48
LANE_BETA = 0
LANE_G = 8

TM = 512
TF = 512
TN = 896
TT_LRU = 640
VMEM_LIMIT = 56 * 1024 * 1024


def _rms(x, w):
    return x * lax.rsqrt(jnp.mean(x * x, axis=-1, keepdims=True) + EPS) * w


def _silu(x):
    return x * jax.nn.sigmoid(x)


def _softplus(x):
    return jnp.maximum(x, 0.0) + jnp.log1p(jnp.exp(-jnp.abs(x)))


def _dot(a, b):
    return jnp.dot(a, b, preferred_element_type=F32)


def _ffn_kernel(x_ref, nw_ref, wg_ref, wu_ref, wd_ref, fw_ref, o_ref, h_ref, *, final_norm):
    j = pl.program_id(1)

    @pl.when(j == 0)
    def _():
        x = x_ref[...]
        h_ref[...] = _rms(x, nw_ref[...]).astype(BF16)
        o_ref[...] = x

    h = h_ref[...]
    g = _dot(h, wg_ref[...])
    u = _dot(h, wu_ref[...])
    a = (_silu(g) * u * 0.5).astype(BF16)
    o_ref[...] += _dot(a, wd_ref[...])

    if final_norm:
        @pl.when(j == pl.num_programs(1) - 1)
        def _():
            o_ref[...] = _rms(o_ref[...], fw_ref[...])


def _ffn(x, nw, wg, wu, wd, fw, final_norm):
    return pl.pallas_call(
        functools.partial(_ffn_kernel, final_norm=final_norm),
        grid=(ROWS // TM, D_FF // TF),
        in_specs=[
            pl.BlockSpec((TM, D_MODEL), lambda i, j: (i, 0)),
            pl.BlockSpec((1, D_MODEL), lambda i, j: (0, 0)),
            pl.BlockSpec((D_MODEL, TF), lambda i, j: (0, j)),
            pl.BlockSpec((D_MODEL, TF), lambda i, j: (0, j)),
            pl.BlockSpec((TF, D_MODEL), lambda i, j: (j, 0)),
            pl.BlockSpec((1, D_MODEL), lambda i, j: (0, 0)),
        ],
        out_specs=pl.BlockSpec((TM, D_MODEL), lambda i, j: (i, 0)),
        out_shape=jax.ShapeDtypeStruct((ROWS, D_MODEL), F32),
        scratch_shapes=[pltpu.VMEM((TM, D_MODEL), BF16)],
        compiler_params=pltpu.CompilerParams(
            dimension_semantics=("parallel", "arbitrary"), vmem_limit_bytes=VMEM_LIMIT),
        name="ffn",
    )(x, nw, wg, wu, wd, fw)


def _inproj_kernel(x_ref, nw_ref, w_ref, o_ref, h_ref):
    @pl.when(pl.program_id(1) == 0)
    def _():
        h_ref[...] = _rms(x_ref[...], nw_ref[...]).astype(BF16)

    o_ref[...] = _dot(h_ref[...], w_ref[...])


def _inproj(x, nw, w):
    return pl.pallas_call(
        _inproj_kernel,
        grid=(ROWS // TM, N_IN_PAD // TN),
        in_specs=[
            pl.BlockSpec((TM, D_MODEL), lambda i, j: (i, 0)),
            pl.BlockSpec((1, D_MODEL), lambda i, j: (0, 0)),
            pl.BlockSpec((D_MODEL, TN), lambda i, j: (0, j)),
        ],
        out_specs=pl.BlockSpec((TM, TN), lambda i, j: (i, j)),
        out_shape=jax.ShapeDtypeStruct((ROWS, N_IN_PAD), F32),
        scratch_shapes=[pltpu.VMEM((TM, D_MODEL), BF16)],
        compiler_params=pltpu.CompilerParams(
            dimension_semantics=("parallel", "arbitrary"), vmem_limit_bytes=VMEM_LIMIT),
        name="inproj",
    )(x, nw, w)


def _outproj_kernel(x_ref, ya_ref, yb_ref, w_ref, o_ref):
    o_ref[...] = (x_ref[...] + _dot(ya_ref[...], w_ref[0:LRU_WIDTH, :])
                  + _dot(yb_ref[...], w_ref[LRU_WIDTH:, :]))


def _outproj(x, ya, yb, w):
    return pl.pallas_call(
        _outproj_kernel,
        grid=(ROWS // TM,),
        in_specs=[
            pl.BlockSpec((TM, D_MODEL), lambda i: (i, 0)),
            pl.BlockSpec((TM, LRU_WIDTH), lambda i: (i, 0)),
            pl.BlockSpec((TM, GDN_VW), lambda i: (i, 0)),
            pl.BlockSpec((D_MODEL, D_MODEL), lambda i: (0, 0)),
        ],
        out_specs=pl.BlockSpec((TM, D_MODEL), lambda i: (i, 0)),
        out_shape=jax.ShapeDtypeStruct((ROWS, D_MODEL), F32),
        compiler_params=pltpu.CompilerParams(
            dimension_semantics=("parallel",), vmem_limit_bytes=VMEM_LIMIT),
        name="outproj",
    )(x, ya, yb, w)


def _conv_from_scratch(xe_ref, cw, tt):
    y = cw[0:1] * xe_ref[5:5 + tt, :]
    for i in range(1, CONV_W):
        y = y + cw[i:i + 1] * xe_ref[5 + i:5 + i + tt, :]
    return y


def _lru_kernel(xa_ref, ga_ref, halo_ref, h0_ref, cw_ref, cb_ref, wg_ref, rgb_ref, igb_ref, lam_ref, na_ref,
                ya_ref, last_ref, xe_ref, a_ref, b_ref, hc_ref, *, tt, last_tile, last_row):
    t = pl.program_id(1)

    @pl.when(t == 0)
    def _():
        xe_ref[0:SUBLANES, :] = halo_ref[0]
        hc_ref[...] = h0_ref[0]

    @pl.when(t > 0)
    def _():
        xe_ref[0:SUBLANES, :] = xe_ref[tt:tt + SUBLANES, :]

    xe_ref[SUBLANES:tt + SUBLANES, :] = xa_ref[...]
    xc = _conv_from_scratch(xe_ref, cw_ref[...], tt) + cb_ref[...]

    gw = 4 * LRU_BLOCK
    r_parts, i_parts = [], []
    for q in range(LRU_WIDTH // gw):
        gg = _dot(xc[:, q * gw:(q + 1) * gw].astype(BF16), wg_ref[q])
        r_parts.append(gg[:, :gw])
        i_parts.append(gg[:, gw:])
    r = jax.nn.sigmoid(jnp.concatenate(r_parts, axis=1) + rgb_ref[...])
    ig = jax.nn.sigmoid(jnp.concatenate(i_parts, axis=1) + igb_ref[...])
    log_a = (-LRU_C) * r * _softplus(-lam_ref[...])
    a = jnp.exp(log_a)
    a_ref[...] = a
    b_ref[...] = jnp.sqrt(-jnp.tanh(log_a) * (a * a + 1.0)) * ig * xc

    row = lax.broadcasted_iota(jnp.int32, (SUBLANES, LRU_WIDTH), 0)

    def group(gi, hc):
        off = pl.multiple_of(gi * SUBLANES, SUBLANES)
        a8 = a_ref[pl.ds(off, SUBLANES), :]
        b8 = b_ref[pl.ds(off, SUBLANES), :]
        for k in (1, 2, 4):
            keep = row >= k
            a_prev = jnp.where(keep, pltpu.roll(a8, k, 0), 1.0)
            b_prev = jnp.where(keep, pltpu.roll(b8, k, 0), 0.0)
            b8 = a8 * b_prev + b8
            a8 = a8 * a_prev
        h8 = a8 * hc + b8
        a_ref[pl.ds(off, SUBLANES), :] = h8
        return h8[SUBLANES - 1:SUBLANES, :]

    hc_ref[...] = lax.fori_loop(0, tt // SUBLANES, group, hc_ref[...])

    h = a_ref[...]
    ya_ref[...] = (_rms(h, na_ref[...]) * jax.nn.gelu(ga_ref[...], approximate=True)).astype(BF16)

    @pl.when(t == last_tile)
    def _():
        last_ref[0] = a_ref[last_row:last_row + 1, :]


def _lru(proj, halo, h0, cw, cb, wg, rgb, igb, lam, na, *, nb, tt, nt, row_block0, last_tile, last_row):
    row_map = lambda col: (lambda b, t: (row_block0 + b * nt + t, col))
    const2 = lambda b, t: (0, 0)
    return pl.pallas_call(
        functools.partial(_lru_kernel, tt=tt, last_tile=last_tile, last_row=last_row),
        grid=(nb, nt),
        in_specs=[
            pl.BlockSpec((tt, LRU_WIDTH), row_map(COL_XA)),
            pl.BlockSpec((tt, LRU_WIDTH), row_map(COL_GA)),
            pl.BlockSpec((1, SUBLANES, LRU_WIDTH), lambda b, t: (b, 0, 0)),
            pl.BlockSpec((1, 1, LRU_WIDTH), lambda b, t: (b, 0, 0)),
            pl.BlockSpec((SUBLANES, LRU_WIDTH), const2),
            pl.BlockSpec((1, LRU_WIDTH), const2),
            pl.BlockSpec((4, 4 * LRU_BLOCK, 8 * LRU_BLOCK), lambda b, t: (0, 0, 0)),
            pl.BlockSpec((1, LRU_WIDTH), const2),
            pl.BlockSpec((1, LRU_WIDTH), const2),
            pl.BlockSpec((1, LRU_WIDTH), const2),
            pl.BlockSpec((1, LRU_WIDTH), const2),
        ],
        out_specs=[
            pl.BlockSpec((tt, LRU_WIDTH), lambda b, t: (b * nt + t, 0)),
            pl.BlockSpec((1, 1, LRU_WIDTH), lambda b, t: (b, 0, 0)),
        ],
        out_shape=[
            jax.ShapeDtypeStruct((nb * nt * tt, LRU_WIDTH), BF16),
            jax.ShapeDtypeStruct((nb, 1, LRU_WIDTH), F32),
        ],
        scratch_shapes=[
            pltpu.VMEM((tt + SUBLANES, LRU_WIDTH), F32),
            pltpu.VMEM((tt, LRU_WIDTH), F32),
            pltpu.VMEM((tt, LRU_WIDTH), F32),
            pltpu.VMEM((1, LRU_WIDTH), F32),
        ],
        compiler_params=pltpu.CompilerParams(
            dimension_semantics=("parallel", "arbitrary"), vmem_limit_bytes=VMEM_LIMIT),
        name="rglru",
    )(proj, proj, halo, h0, cw, cb, wg, rgb, igb, lam, na)


def _split_bf16(x):
    hi = x.astype(BF16)
    lo = (x - hi.astype(F32)).astype(BF16)
    return hi, lo


def _pdot(x, y):
    xh, xl = _split_bf16(x)
    yh, yl = _split_bf16(y)
    lhs = jnp.concatenate([xh, xl, xh], axis=1)
    rhs = jnp.concatenate([yh, yh, yl], axis=0)
    return _dot(lhs, rhs)


def _unit_lower_inverse(a, ri, ci):
    eye = (ri == ci).astype(F32)
    a0 = jnp.where((ri // SUBLANES) == (ci // SUBLANES), a, 0.0)
    p2 = _pdot(a0, a0)
    p4 = _pdot(p2, p2)
    t = _pdot(_pdot(eye - a0, eye + p2), eye + p4)
    m = SUBLANES
    while m < CHUNK:
        e = jnp.where(((ri // (2 * m)) == (ci // (2 * m))) & ((ri // m) != (ci // m)), a, 0.0)
        t = t - _pdot(t, _pdot(e, t))
        m *= 2
    return t


def _pad_rows(x, rows):
    if x.shape[0] == rows:
        return x
    return jnp.concatenate([x, jnp.zeros((rows - x.shape[0], x.shape[1]), x.dtype)], axis=0)


def _gdn_prep_kernel(q_ref, k_ref, v_ref, tail_ref, halo_ref, cw_ref, arow_ref, dtrow_ref,
                     lhs1_ref, lhs2_ref, u_ref, ge_ref, xe_ref, *, tr, valid):
    c = pl.program_id(1)
    srcs = (q_ref, k_ref, v_ref)

    @pl.when(c == 0)
    def _():
        for i in range(3):
            xe_ref[i, 0:SUBLANES, :] = halo_ref[0, :, i * GDN_QK:(i + 1) * GDN_QK]

    @pl.when(c > 0)
    def _():
        for i in range(3):
            xe_ref[i, 0:SUBLANES, :] = xe_ref[i, tr:tr + SUBLANES, :]

    qkv = []
    for i in range(3):
        xe_ref[i, SUBLANES:tr + SUBLANES, :] = srcs[i][...]
        y = _conv_from_scratch(xe_ref.at[i], cw_ref[:, i * GDN_QK:(i + 1) * GDN_QK], tr)
        qkv.append(_pad_rows(_silu(y), CHUNK))
    q_all, k_all, v_all = qkv

    tail = tail_ref[...]
    rows_left = valid - c * tr
    live = (lax.broadcasted_iota(jnp.int32, (tr, LANES), 0) < rows_left).astype(F32)
    beta = _pad_rows(jax.nn.sigmoid(tail) * live, CHUNK)
    g = _pad_rows(-jnp.exp(arow_ref[...]) * _softplus(tail + dtrow_ref[...]) * live, CHUNK)

    ri = lax.broadcasted_iota(jnp.int32, (CHUNK, CHUNK), 0)
    ci = lax.broadcasted_iota(jnp.int32, (CHUNK, CHUNK), 1)
    incl = ri >= ci
    strict = ri > ci
    gc = jnp.dot(incl.astype(F32), g, precision=lax.Precision.HIGHEST, preferred_element_type=F32)
    gc_t = gc.T
    ge_ref[0, 0] = jnp.broadcast_to(
        jnp.exp(gc_t[LANE_G:LANE_G + GDN_HEADS, CHUNK - 1:CHUNK]), (GDN_HEADS, LANES))

    for h in range(GDN_HEADS):
        sl = slice(h * GDN_DK, (h + 1) * GDN_DK)
        qh, kh, vh = q_all[:, sl], k_all[:, sl], v_all[:, sl]
        qh = qh * lax.rsqrt(jnp.sum(qh * qh, axis=-1, keepdims=True) + EPS) * (GDN_DK ** -0.5)
        kh = kh * lax.rsqrt(jnp.sum(kh * kh, axis=-1, keepdims=True) + EPS)
        gcol = gc[:, LANE_G + h:LANE_G + h + 1]
        grow = gc_t[LANE_G + h:LANE_G + h + 1, :]
        bcol = beta[:, LANE_BETA + h:LANE_BETA + h + 1]
        decay = jnp.where(incl, jnp.exp(jnp.where(incl, gcol - grow, 0.0)), 0.0)
        kb = kh.astype(BF16)
        qkk = lax.dot_general(jnp.concatenate([qh.astype(BF16), kb], axis=0), kb,
                              (((1,), (1,)), ((), ())), preferred_element_type=F32)
        qk = qkk[:CHUNK] * decay
        a = jnp.where(strict, bcol * decay * qkk[CHUNK:], 0.0)
        t = _unit_lower_inverse(a, ri, ci)
        egc = jnp.exp(gcol)
        sol = _pdot(t, jnp.concatenate([bcol * vh, (bcol * egc) * kh], axis=1))
        u_ref[0, 0, h] = sol[:, :GDN_DV]
        lhs1_ref[0, 0, h, 0:CHUNK, :] = sol[:, GDN_DV:].astype(BF16)
        lhs1_ref[0, 0, h, CHUNK:, :] = (qh * egc).astype(BF16)
        k_end = kh * jnp.exp(grow[:, CHUNK - 1:CHUNK] - gcol)
        lhs2_ref[0, 0, h, 0:CHUNK, :] = qk.astype(BF16)
        lhs2_ref[0, 0, h, CHUNK:, :] = k_end.T.astype(BF16)


def _gdn_prep(proj, halo, cw, arow, dtrow, *, nb, tr, nc, row_block0, valid):
    row_map = lambda col: (lambda b, c: (row_block0 + b * nc + c, col))
    const2 = lambda b, c: (0, 0)
    blk5 = lambda b, c: (b, c, 0, 0, 0)
    return pl.pallas_call(
        functools.partial(_gdn_prep_kernel, tr=tr, valid=valid),
        grid=(nb, nc),
        in_specs=[
            pl.BlockSpec((tr, GDN_QK), row_map(COL_Q)),
            pl.BlockSpec((tr, GDN_QK), row_map(COL_K)),
            pl.BlockSpec((tr, GDN_VW), row_map(COL_V)),
            pl.BlockSpec((tr, LANES), row_map(COL_TAIL)),
            pl.BlockSpec((1, SUBLANES, GDN_QKV), lambda b, c: (b, 0, 0)),
            pl.BlockSpec((SUBLANES, GDN_QKV), const2),
            pl.BlockSpec((1, LANES), const2),
            pl.BlockSpec((1, LANES), const2),
        ],
        out_specs=[
            pl.BlockSpec((1, 1, GDN_HEADS, 2 * CHUNK, GDN_DK), blk5),
            pl.BlockSpec((1, 1, GDN_HEADS, 2 * CHUNK, GDN_DK), blk5),
            pl.BlockSpec((1, 1, GDN_HEADS, CHUNK, GDN_DV), blk5),
            pl.BlockSpec((1, 1, GDN_HEADS, LANES), lambda b, c: (b, c, 0, 0)),
        ],
        out_shape=[
            jax.ShapeDtypeStruct((nb, nc, GDN_HEADS, 2 * CHUNK, GDN_DK), BF16),
            jax.ShapeDtypeStruct((nb, nc, GDN_HEADS, 2 * CHUNK, GDN_DK), BF16),
            jax.ShapeDtypeStruct((nb, nc, GDN_HEADS, CHUNK, GDN_DV), F32),
            jax.ShapeDtypeStruct((nb, nc, GDN_HEADS, LANES), F32),
        ],
        scratch_shapes=[pltpu.VMEM((3, tr + SUBLANES, GDN_QK), F32)],
        compiler_params=pltpu.CompilerParams(
            dimension_semantics=("parallel", "arbitrary"), vmem_limit_bytes=VMEM_LIMIT),
        name="gdn_prep",
    )(proj, proj, proj, proj, halo, cw, arow, dtrow)


def _gdn_scan_kernel(lhs1_ref, lhs2_ref, u_ref, ge_ref, z_ref, s0_ref, nb_ref, yb_ref, sout_ref, s_ref, *, tr):
    c = pl.program_id(1)

    @pl.when(c == 0)
    def _():
        s_ref[...] = s0_ref[0]

    for h in range(GDN_HEADS):
        s = s_ref[h]
        m1 = _dot(lhs1_ref[0, 0, h], s.astype(BF16))
        w = u_ref[0, 0, h] - m1[:CHUNK]
        m2 = _dot(lhs2_ref[0, 0, h], w.astype(BF16))
        o = (m1[CHUNK:] + m2[:CHUNK])[:tr]
        s_ref[h] = ge_ref[0, 0, h:h + 1, :] * s + m2[CHUNK:]
        sl = slice(h * GDN_DV, (h + 1) * GDN_DV)
        yb_ref[:, sl] = (_rms(o, nb_ref[...]) * _silu(z_ref[:, sl])).astype(BF16)

    @pl.when(c == pl.num_programs(1) - 1)
    def _():
        sout_ref[0] = s_ref[...]


def _gdn_scan(lhs1, lhs2, u, ge, proj, s0, nbw, *, nb, tr, nc, row_block0):
    blk5 = lambda b, c: (b, c, 0, 0, 0)
    return pl.pallas_call(
        functools.partial(_gdn_scan_kernel, tr=tr),
        grid=(nb, nc),
        in_specs=[
            pl.BlockSpec((1, 1, GDN_HEADS, 2 * CHUNK, GDN_DK), blk5),
            pl.BlockSpec((1, 1, GDN_HEADS, 2 * CHUNK, GDN_DK), blk5),
            pl.BlockSpec((1, 1, GDN_HEADS, CHUNK, GDN_DV), blk5),
            pl.BlockSpec((1, 1, GDN_HEADS, LANES), lambda b, c: (b, c, 0, 0)),
            pl.BlockSpec((tr, GDN_VW), lambda b, c: (row_block0 + b * nc + c, COL_Z)),
            pl.BlockSpec((1, GDN_HEADS, GDN_DK, GDN_DV), lambda b, c: (b, 0, 0, 0)),
            pl.BlockSpec((1, GDN_DV), lambda b, c: (0, 0)),
        ],
        out_specs=[
            pl.BlockSpec((tr, GDN_VW), lambda b, c: (b * nc + c, 0)),
            pl.BlockSpec((1, GDN_HEADS, GDN_DK, GDN_DV), lambda b, c: (b, 0, 0, 0)),
        ],
        out_shape=[
            jax.ShapeDtypeStruct((nb * nc * tr, GDN_VW), BF16),
            jax.ShapeDtypeStruct((nb, GDN_HEADS, GDN_DK, GDN_DV), F32),
        ],
        scratch_shapes=[pltpu.VMEM((GDN_HEADS, GDN_DK, GDN_DV), F32)],
        compiler_params=pltpu.CompilerParams(
            dimension_semantics=("parallel", "arbitrary"), vmem_limit_bytes=VMEM_LIMIT),
        name="gdn_scan",
    )(lhs1, lhs2, u, ge, proj, s0, nbw)


def _halo(hist):
    return jnp.pad(hist, ((0, 0), (SUBLANES - (CONV_W - 1), 0), (0, 0)))


def _taps(w):
    return jnp.pad(w, ((0, SUBLANES - CONV_W), (0, 0)))


def _gate_weights(rg_w, ig_w):
    eye = jnp.eye(4, dtype=F32)

    def bd(w):
        return jnp.einsum('qnij,nm->qnimj', w.reshape(4, 4, LRU_BLOCK, LRU_BLOCK), eye).reshape(
            4, 4 * LRU_BLOCK, 4 * LRU_BLOCK)

    return jnp.concatenate([bd(rg_w), bd(ig_w)], axis=2).astype(BF16)


def _lane_row(vals, lane0):
    return jnp.zeros((1, LANES), F32).at[0, lane0:lane0 + vals.shape[0]].set(vals)


def _mixer(x, l, st, p):
    state_conv_a, state_lru, state_conv_b, state_delta = st
    w_in = jnp.pad(p['w_in'][l].astype(BF16), ((0, 0), (0, N_IN_PAD - N_IN)))
    proj = _inproj(x, p['mix_norm'][l][None], w_in)

    cw_a, cb_a = _taps(p['conv_a_w'][l]), p['conv_a_b'][l][None]
    wg = _gate_weights(p['rg_w'][l], p['ig_w'][l])
    lru_args = (cw_a, cb_a, wg, p['rg_b'][l][None], p['ig_b'][l][None], p['lru_lambda'][l][None],
                p['norm_a'][l][None])
    cw_b = _taps(p['conv_b_w'][l])
    arow = _lane_row(p['a_log'][l], LANE_G)
    dtrow = _lane_row(p['dt_bias'][l], LANE_G)
    nbw = p['norm_b'][l][None]

    nt = TP // TT_LRU
    ya_p, lru_p = _lru(proj, jnp.zeros((BATCH, SUBLANES, LRU_WIDTH), F32), jnp.zeros((BATCH, 1, LRU_WIDTH), F32),
                       *lru_args, nb=BATCH, tt=TT_LRU, nt=nt, row_block0=0,
                       last_tile=(VALID_P - 1) // TT_LRU, last_row=(VALID_P - 1) % TT_LRU)
    ncp = TP // CHUNK
    pre = _gdn_prep(proj, jnp.zeros((BATCH, SUBLANES, GDN_QKV), F32), cw_b, arow, dtrow,
                    nb=BATCH, tr=CHUNK, nc=ncp, row_block0=0, valid=VALID_P)
    yb_p, dl_p = _gdn_scan(*pre, proj, jnp.zeros((BATCH, GDN_HEADS, GDN_DK, GDN_DV), F32), nbw,
                           nb=BATCH, tr=CHUNK, nc=ncp, row_block0=0)

    rb0 = ROWS_P // DEC_SEQ
    ya_s, lru_s = _lru(proj, _halo(state_conv_a[l]), state_lru[l][:, None, :], *lru_args,
                       nb=DEC_BATCH, tt=DEC_SEQ, nt=1, row_block0=rb0, last_tile=0, last_row=DEC_SEQ - 1)
    pre = _gdn_prep(proj, _halo(state_conv_b[l]), cw_b, arow, dtrow,
                    nb=DEC_BATCH, tr=DEC_SEQ, nc=1, row_block0=rb0, valid=DEC_SEQ)
    yb_s, dl_s = _gdn_scan(*pre, proj, state_delta[l], nbw, nb=DEC_BATCH, tr=DEC_SEQ, nc=1, row_block0=rb0)

    ya = jnp.concatenate([ya_p, ya_s], axis=0)
    yb = jnp.concatenate([yb_p, yb_s], axis=0)
    x = _outproj(x, ya, yb, p['w_out'][l].astype(BF16))

    pp = proj[:ROWS_P].reshape(BATCH, TP, N_IN_PAD)[:, VALID_P - 3:VALID_P]
    ps = proj[ROWS_P:].reshape(DEC_BATCH, DEC_SEQ, N_IN_PAD)[:, DEC_SEQ - 3:]
    o2 = 2 * LRU_WIDTH
    new_p = (pp[..., :LRU_WIDTH], lru_p[:, 0], pp[..., o2:o2 + GDN_QKV], dl_p)
    new_s = (ps[..., :LRU_WIDTH], lru_s[:, 0], ps[..., o2:o2 + GDN_QKV], dl_s)
    return x, new_p, new_s


def kernel(x_prompt, x_sample, state_conv_a, state_lru, state_conv_b, state_delta, meta_tokens, ffn1_norm, ffn1_w_gate, ffn1_w_up, ffn1_w_down, mix_norm, w_in, conv_a_w, conv_a_b, rg_w, rg_b, ig_w, ig_b, lru_lambda, norm_a, conv_b_w, a_log, dt_bias, norm_b, w_out, ffn2_norm, ffn2_w_gate, ffn2_w_up, ffn2_w_down, final_norm):
    p = dict(mix_norm=mix_norm, w_in=w_in, conv_a_w=conv_a_w, conv_a_b=conv_a_b, rg_w=rg_w, rg_b=rg_b,
             ig_w=ig_w, ig_b=ig_b, lru_lambda=lru_lambda, norm_a=norm_a, conv_b_w=conv_b_w, a_log=a_log,
             dt_bias=dt_bias, norm_b=norm_b, w_out=w_out)
    meta = jnp.broadcast_to(meta_tokens[None], (BATCH, N_META, D_MODEL))
    pad = jnp.zeros((BATCH, TP - VALID_P, D_MODEL), F32)
    x = jnp.concatenate([jnp.concatenate([meta, x_prompt, pad], axis=1).reshape(ROWS_P, D_MODEL),
                         x_sample.reshape(ROWS_S, D_MODEL)], axis=0)
    st = (state_conv_a, state_lru, state_conv_b, state_delta)
    fw = final_norm[None]
    news_p, news_s = [], []
    for l in range(DEPTH):
        x = _ffn(x, ffn1_norm[l][None], ffn1_w_gate[l].astype(BF16), ffn1_w_up[l].astype(BF16),
                 ffn1_w_down[l].astype(BF16), fw, False)
        x, new_p, new_s = _mixer(x, l, st, p)
        x = _ffn(x, ffn2_norm[l][None], ffn2_w_gate[l].astype(BF16), ffn2_w_up[l].astype(BF16),
                 ffn2_w_down[l].astype(BF16), fw, l == DEPTH - 1)
        news_p.append(new_p)
        news_s.append(new_s)
    y_p = x[:ROWS_P].reshape(BATCH, TP, D_MODEL)[:, N_META:VALID_P]
    y_s = x[ROWS_P:].reshape(DEC_BATCH, DEC_SEQ, D_MODEL)
    stack = lambda news, i: jnp.stack([n[i] for n in news], axis=0)
    return (y_p, y_s,
            stack(news_p, 0), stack(news_p, 1), stack(news_p, 2), stack(news_p, 3),
            stack(news_s, 0), stack(news_s, 1), stack(news_s, 2), stack(news_s, 3))
```

```python
import functools

import jax
import jax.numpy as jnp
from jax import lax
from jax.experimental import pallas as pl
from jax.experimental.pallas import tpu as pltpu

F32 = jnp.float32
BF16 = jnp.bfloat16

D_MODEL = 2048
BATCH = 2
SEQ = 8192
DEPTH = 2
DEC_BATCH = 16
DEC_SEQ = 16
N_META = 16
LRU_WIDTH = 1024
LRU_BLOCKS = 16
LRU_BLOCK = 64
LRU_C = 8.0
CONV_W = 4
GDN_HEADS = 8
GDN_DK = 128
GDN_DV = 128
GDN_QK = 1024
GDN_VW = 1024
GDN_QKV = 3072
N_IN = 6160
D_FF = 5632
EPS = 1e-6

LANES = 128
SUBLANES = 8
CHUNK = 128
VALID_P = N_META + SEQ
TP = 8320
ROWS_P = BATCH * TP
ROWS_S = DEC_BATCH * DEC_SEQ
ROWS = ROWS_P + ROWS_S
N_IN_PAD = 6272
COL_XA, COL_GA, COL_Q, COL_K, COL_V, COL_Z = 0, 1, 2, 3, 4, 5
COL_TAIL = 48
LANE_BETA = 0
LANE_G = 8

TM = 768
TF = 512
TN = 896
TT_LRU = 640
VMEM_LIMIT = 56 * 1024 * 1024


def _rms(x, w):
    return x * lax.rsqrt(jnp.mean(x * x, axis=-1, keepdims=True) + EPS) * w


def _silu(x):
    return x * jax.nn.sigmoid(x)


def _softplus(x):
    return jnp.maximum(x, 0.0) + jnp.log1p(jnp.exp(-jnp.abs(x)))


def _dot(a, b):
    return jnp.dot(a, b, preferred_element_type=F32)


def _ffn_kernel(x_ref, nw_ref, wg_ref, wu_ref, wd_ref, fw_ref, o_ref, h_ref, *, final_norm):
    j = pl.program_id(1)

    @pl.when(j == 0)
    def _():
        x = x_ref[...]
        h_ref[...] = _rms(x, nw_ref[...]).astype(BF16)
        o_ref[...] = x

    h = h_ref[...]
    g = _dot(h, wg_ref[...])
    u = _dot(h, wu_ref[...])
    a = (_silu(g) * u * 0.5).astype(BF16)
    o_ref[...] += _dot(a, wd_ref[...])

    if final_norm:
        @pl.when(j == pl.num_programs(1) - 1)
        def _():
            o_ref[...] = _rms(o_ref[...], fw_ref[...])


def _ffn(x, nw, wg, wu, wd, fw, final_norm):
    return pl.pallas_call(
        functools.partial(_ffn_kernel, final_norm=final_norm),
        grid=(ROWS // TM, D_FF // TF),
        in_specs=[
            pl.BlockSpec((TM, D_MODEL), lambda i, j: (i, 0)),
            pl.BlockSpec((1, D_MODEL), lambda i, j: (0, 0)),
            pl.BlockSpec((D_MODEL, TF), lambda i, j: (0, j)),
            pl.BlockSpec((D_MODEL, TF), lambda i, j: (0, j)),
            pl.BlockSpec((TF, D_MODEL), lambda i, j: (j, 0)),
            pl.BlockSpec((1, D_MODEL), lambda i, j: (0, 0)),
        ],
        out_specs=pl.BlockSpec((TM, D_MODEL), lambda i, j: (i, 0)),
        out_shape=jax.ShapeDtypeStruct((ROWS, D_MODEL), F32),
        scratch_shapes=[pltpu.VMEM((TM, D_MODEL), BF16)],
        compiler_params=pltpu.CompilerParams(
            dimension_semantics=("parallel", "arbitrary"), vmem_limit_bytes=VMEM_LIMIT),
        name="ffn",
    )(x, nw, wg, wu, wd, fw)


def _inproj_kernel(x_ref, nw_ref, w_ref, o_ref, h_ref):
    @pl.when(pl.program_id(1) == 0)
    def _():
        h_ref[...] = _rms(x_ref[...], nw_ref[...]).astype(BF16)

    o_ref[...] = _dot(h_ref[...], w_ref[...])


def _inproj(x, nw, w):
    return pl.pallas_call(
        _inproj_kernel,
        grid=(ROWS // TM, N_IN_PAD // TN),
        in_specs=[
            pl.BlockSpec((TM, D_MODEL), lambda i, j: (i, 0)),
            pl.BlockSpec((1, D_MODEL), lambda i, j: (0, 0)),
            pl.BlockSpec((D_MODEL, TN), lambda i, j: (0, j)),
        ],
        out_specs=pl.BlockSpec((TM, TN), lambda i, j: (i, j)),
        out_shape=jax.ShapeDtypeStruct((ROWS, N_IN_PAD), F32),
        scratch_shapes=[pltpu.VMEM((TM, D_MODEL), BF16)],
        compiler_params=pltpu.CompilerParams(
            dimension_semantics=("parallel", "arbitrary"), vmem_limit_bytes=VMEM_LIMIT),
        name="inproj",
    )(x, nw, w)


def _outproj_kernel(x_ref, ya_ref, yb_ref, w_ref, o_ref):
    o_ref[...] = (x_ref[...] + _dot(ya_ref[...], w_ref[0:LRU_WIDTH, :])
                  + _dot(yb_ref[...], w_ref[LRU_WIDTH:, :]))


def _outproj(x, ya, yb, w):
    return pl.pallas_call(
        _outproj_kernel,
        grid=(ROWS // TM,),
        in_specs=[
            pl.BlockSpec((TM, D_MODEL), lambda i: (i, 0)),
            pl.BlockSpec((TM, LRU_WIDTH), lambda i: (i, 0)),
            pl.BlockSpec((TM, GDN_VW), lambda i: (i, 0)),
            pl.BlockSpec((D_MODEL, D_MODEL), lambda i: (0, 0)),
        ],
        out_specs=pl.BlockSpec((TM, D_MODEL), lambda i: (i, 0)),
        out_shape=jax.ShapeDtypeStruct((ROWS, D_MODEL), F32),
        compiler_params=pltpu.CompilerParams(
            dimension_semantics=("parallel",), vmem_limit_bytes=VMEM_LIMIT),
        name="outproj",
    )(x, ya, yb, w)


def _conv_from_scratch(xe_ref, cw, tt):
    y = cw[0:1] * xe_ref[5:5 + tt, :]
    for i in range(1, CONV_W):
        y = y + cw[i:i + 1] * xe_ref[5 + i:5 + i + tt, :]
    return y


def _lru_kernel(xa_ref, ga_ref, halo_ref, h0_ref, cw_ref, cb_ref, wg_ref, rgb_ref, igb_ref, lam_ref, na_ref,
                ya_ref, last_ref, xe_ref, a_ref, b_ref, hc_ref, *, tt, last_tile, last_row):
    t = pl.program_id(1)

    @pl.when(t == 0)
    def _():
        xe_ref[0:SUBLANES, :] = halo_ref[0]
        hc_ref[...] = h0_ref[0]

    @pl.when(t > 0)
    def _():
        xe_ref[0:SUBLANES, :] = xe_ref[tt:tt + SUBLANES, :]

    xe_ref[SUBLANES:tt + SUBLANES, :] = xa_ref[...]
    xc = _conv_from_scratch(xe_ref, cw_ref[...], tt) + cb_ref[...]

    gw = 4 * LRU_BLOCK
    r_parts, i_parts = [], []
    for q in range(LRU_WIDTH // gw):
        gg = _dot(xc[:, q * gw:(q + 1) * gw].astype(BF16), wg_ref[q])
        r_parts.append(gg[:, :gw])
        i_parts.append(gg[:, gw:])
    r = jax.nn.sigmoid(jnp.concatenate(r_parts, axis=1) + rgb_ref[...])
    ig = jax.nn.sigmoid(jnp.concatenate(i_parts, axis=1) + igb_ref[...])
    log_a = (-LRU_C) * r * _softplus(-lam_ref[...])
    a = jnp.exp(log_a)
    a_ref[...] = a
    b_ref[...] = jnp.sqrt(-jnp.tanh(log_a) * (a * a + 1.0)) * ig * xc

    row = lax.broadcasted_iota(jnp.int32, (SUBLANES, LRU_WIDTH), 0)

    def group(gi, hc):
        off = pl.multiple_of(gi * SUBLANES, SUBLANES)
        a8 = a_ref[pl.ds(off, SUBLANES), :]
        b8 = b_ref[pl.ds(off, SUBLANES), :]
        for k in (1, 2, 4):
            keep = row >= k
            a_prev = jnp.where(keep, pltpu.roll(a8, k, 0), 1.0)
            b_prev = jnp.where(keep, pltpu.roll(b8, k, 0), 0.0)
            b8 = a8 * b_prev + b8
            a8 = a8 * a_prev
        h8 = a8 * hc + b8
        a_ref[pl.ds(off, SUBLANES), :] = h8
        return h8[SUBLANES - 1:SUBLANES, :]

    hc_ref[...] = lax.fori_loop(0, tt // SUBLANES, group, hc_ref[...])

    h = a_ref[...]
    ya_ref[...] = (_rms(h, na_ref[...]) * jax.nn.gelu(ga_ref[...], approximate=True)).astype(BF16)

    @pl.when(t == last_tile)
    def _():
        last_ref[0] = a_ref[last_row:last_row + 1, :]


def _lru(proj, halo, h0, cw, cb, wg, rgb, igb, lam, na, *, nb, tt, nt, row_block0, last_tile, last_row):
    row_map = lambda col: (lambda b, t: (row_block0 + b * nt + t, col))
    const2 = lambda b, t: (0, 0)
    return pl.pallas_call(
        functools.partial(_lru_kernel, tt=tt, last_tile=last_tile, last_row=last_row),
        grid=(nb, nt),
        in_specs=[
            pl.BlockSpec((tt, LRU_WIDTH), row_map(COL_XA)),
            pl.BlockSpec((tt, LRU_WIDTH), row_map(COL_GA)),
            pl.BlockSpec((1, SUBLANES, LRU_WIDTH), lambda b, t: (b, 0, 0)),
            pl.BlockSpec((1, 1, LRU_WIDTH), lambda b, t: (b, 0, 0)),
            pl.BlockSpec((SUBLANES, LRU_WIDTH), const2),
            pl.BlockSpec((1, LRU_WIDTH), const2),
            pl.BlockSpec((4, 4 * LRU_BLOCK, 8 * LRU_BLOCK), lambda b, t: (0, 0, 0)),
            pl.BlockSpec((1, LRU_WIDTH), const2),
            pl.BlockSpec((1, LRU_WIDTH), const2),
            pl.BlockSpec((1, LRU_WIDTH), const2),
            pl.BlockSpec((1, LRU_WIDTH), const2),
        ],
        out_specs=[
            pl.BlockSpec((tt, LRU_WIDTH), lambda b, t: (b * nt + t, 0)),
            pl.BlockSpec((1, 1, LRU_WIDTH), lambda b, t: (b, 0, 0)),
        ],
        out_shape=[
            jax.ShapeDtypeStruct((nb * nt * tt, LRU_WIDTH), BF16),
            jax.ShapeDtypeStruct((nb, 1, LRU_WIDTH), F32),
        ],
        scratch_shapes=[
            pltpu.VMEM((tt + SUBLANES, LRU_WIDTH), F32),
            pltpu.VMEM((tt, LRU_WIDTH), F32),
            pltpu.VMEM((tt, LRU_WIDTH), F32),
            pltpu.VMEM((1, LRU_WIDTH), F32),
        ],
        compiler_params=pltpu.CompilerParams(
            dimension_semantics=("parallel", "arbitrary"), vmem_limit_bytes=VMEM_LIMIT),
        name="rglru",
    )(proj, proj, halo, h0, cw, cb, wg, rgb, igb, lam, na)


def _split_bf16(x):
    hi = x.astype(BF16)
    lo = (x - hi.astype(F32)).astype(BF16)
    return hi, lo


def _pdot(x, y):
    (xh, xl), (yh, yl) = x, y
    lhs = jnp.concatenate([xh, xl, xh], axis=1)
    rhs = jnp.concatenate([yh, yh, yl], axis=0)
    return _dot(lhs, rhs)


N_LEVELS = 4


def _inverse_masks():
    ri = lax.broadcasted_iota(jnp.int32, (CHUNK, CHUNK), 0)
    ci = lax.broadcasted_iota(jnp.int32, (CHUNK, CHUNK), 1)
    masks = [ri == ci, (ri // SUBLANES) == (ci // SUBLANES)]
    m = SUBLANES
    while m < CHUNK:
        masks.append(((ri // (2 * m)) == (ci // (2 * m))) & ((ri // m) != (ci // m)) & (ri > ci))
        m *= 2
    return jnp.stack(masks, axis=0).astype(BF16)


def _unit_lower_inverses(a_list, mask_ref):
    eye = mask_ref[0]
    blk = mask_ref[1]
    a_s = [_split_bf16(a) for a in a_list]
    a0_s = [(ah * blk, al * blk) for ah, al in a_s]
    p2_s = [_split_bf16(_pdot(a0, a0)) for a0 in a0_s]
    p4_s = [_split_bf16(_pdot(p2, p2)) for p2 in p2_s]
    t1_s = [_split_bf16(_pdot((eye - a0h, -a0l), (eye + p2h, p2l))) for (a0h, a0l), (p2h, p2l) in zip(a0_s, p2_s)]
    t_list = [_pdot(t1, (eye + p4h, p4l)) for t1, (p4h, p4l) in zip(t1_s, p4_s)]
    for lvl in range(N_LEVELS):
        sub = mask_ref[2 + lvl]
        t_s = [_split_bf16(t) for t in t_list]
        et_s = [_split_bf16(_pdot((ah * sub, al * sub), ts)) for (ah, al), ts in zip(a_s, t_s)]
        t_list = [t - _pdot(ts, et) for t, ts, et in zip(t_list, t_s, et_s)]
    return t_list


def _pad_rows(x, rows):
    if x.shape[0] == rows:
        return x
    return jnp.concatenate([x, jnp.zeros((rows - x.shape[0], x.shape[1]), x.dtype)], axis=0)


def _gdn_prep_kernel(q_ref, k_ref, v_ref, tail_ref, halo_ref, cw_ref, arow_ref, dtrow_ref, mask_ref,
                     lhs1_ref, lhs2_ref, u_ref, ge_ref, xe_ref, *, tr, valid):
    c = pl.program_id(1)
    srcs = (q_ref, k_ref, v_ref)

    @pl.when(c == 0)
    def _():
        for i in range(3):
            xe_ref[i, 0:SUBLANES, :] = halo_ref[0, :, i * GDN_QK:(i + 1) * GDN_QK]

    @pl.when(c > 0)
    def _():
        for i in range(3):
            xe_ref[i, 0:SUBLANES, :] = xe_ref[i, tr:tr + SUBLANES, :]

    qkv = []
    for i in range(3):
        xe_ref[i, SUBLANES:tr + SUBLANES, :] = srcs[i][...]
        y = _conv_from_scratch(xe_ref.at[i], cw_ref[:, i * GDN_QK:(i + 1) * GDN_QK], tr)
        qkv.append(_pad_rows(_silu(y), CHUNK))
    q_all, k_all, v_all = qkv

    tail = tail_ref[...]
    rows_left = valid - c * tr
    live = (lax.broadcasted_iota(jnp.int32, (tr, LANES), 0) < rows_left).astype(F32)
    beta = _pad_rows(jax.nn.sigmoid(tail) * live, CHUNK)
    g = _pad_rows(-jnp.exp(arow_ref[...]) * _softplus(tail + dtrow_ref[...]) * live, CHUNK)

    ri = lax.broadcasted_iota(jnp.int32, (CHUNK, CHUNK), 0)
    ci = lax.broadcasted_iota(jnp.int32, (CHUNK, CHUNK), 1)
    incl = ri >= ci
    strict = ri > ci
    gc = jnp.dot(incl.astype(F32), g, precision=lax.Precision.HIGHEST, preferred_element_type=F32)
    gc_t = gc.T
    ge_ref[0, 0] = jnp.broadcast_to(
        jnp.exp(gc_t[LANE_G:LANE_G + GDN_HEADS, CHUNK - 1:CHUNK]), (GDN_HEADS, LANES))

    a_list, rhs_list = [], []
    for h in range(GDN_HEADS):
        sl = slice(h * GDN_DK, (h + 1) * GDN_DK)
        qh, kh, vh = q_all[:, sl], k_all[:, sl], v_all[:, sl]
        qh = qh * lax.rsqrt(jnp.sum(qh * qh, axis=-1, keepdims=True) + EPS) * (GDN_DK ** -0.5)
        kh = kh * lax.rsqrt(jnp.sum(kh * kh, axis=-1, keepdims=True) + EPS)
        gcol = gc[:, LANE_G + h:LANE_G + h + 1]
        grow = gc_t[LANE_G + h:LANE_G + h + 1, :]
        bcol = beta[:, LANE_BETA + h:LANE_BETA + h + 1]
        decay = jnp.where(incl, jnp.exp(jnp.where(incl, gcol - grow, 0.0)), 0.0)
        kb = kh.astype(BF16)
        qkk = lax.dot_general(jnp.concatenate([qh.astype(BF16), kb], axis=0), kb,
                              (((1,), (1,)), ((), ())), preferred_element_type=F32)
        a_list.append(jnp.where(strict, bcol * decay * qkk[CHUNK:], 0.0))
        egc = jnp.exp(gcol)
        rhs_list.append(jnp.concatenate([bcol * vh, (bcol * egc) * kh], axis=1))
        lhs1_ref[0, 0, h, CHUNK:, :] = (qh * egc).astype(BF16)
        k_end = kh * jnp.exp(grow[:, CHUNK - 1:CHUNK] - gcol)
        lhs2_ref[0, 0, h, 0:CHUNK, :] = (qkk[:CHUNK] * decay).astype(BF16)
        lhs2_ref[0, 0, h, CHUNK:, :] = k_end.T.astype(BF16)

    t_list = _unit_lower_inverses(a_list, mask_ref)
    for h in range(GDN_HEADS):
        sol = _pdot(_split_bf16(t_list[h]), _split_bf16(rhs_list[h]))
        u_ref[0, 0, h] = sol[:, :GDN_DV]
        lhs1_ref[0, 0, h, 0:CHUNK, :] = sol[:, GDN_DV:].astype(BF16)


def _gdn_prep(proj, halo, cw, arow, dtrow, *, nb, tr, nc, row_block0, valid):
    row_map = lambda col: (lambda b, c: (row_block0 + b * nc + c, col))
    const2 = lambda b, c: (0, 0)
    blk5 = lambda b, c: (b, c, 0, 0, 0)
    return pl.pallas_call(
        functools.partial(_gdn_prep_kernel, tr=tr, valid=valid),
        grid=(nb, nc),
        in_specs=[
            pl.BlockSpec((tr, GDN_QK), row_map(COL_Q)),
            pl.BlockSpec((tr, GDN_QK), row_map(COL_K)),
            pl.BlockSpec((tr, GDN_VW), row_map(COL_V)),
            pl.BlockSpec((tr, LANES), row_map(COL_TAIL)),
            pl.BlockSpec((1, SUBLANES, GDN_QKV), lambda b, c: (b, 0, 0)),
            pl.BlockSpec((SUBLANES, GDN_QKV), const2),
            pl.BlockSpec((1, LANES), const2),
            pl.BlockSpec((1, LANES), const2),
            pl.BlockSpec((2 + N_LEVELS, CHUNK, CHUNK), lambda b, c: (0, 0, 0)),
        ],
        out_specs=[
            pl.BlockSpec((1, 1, GDN_HEADS, 2 * CHUNK, GDN_DK), blk5),
            pl.BlockSpec((1, 1, GDN_HEADS, 2 * CHUNK, GDN_DK), blk5),
            pl.BlockSpec((1, 1, GDN_HEADS, CHUNK, GDN_DV), blk5),
            pl.BlockSpec((1, 1, GDN_HEADS, LANES), lambda b, c: (b, c, 0, 0)),
        ],
        out_shape=[
            jax.ShapeDtypeStruct((nb, nc, GDN_HEADS, 2 * CHUNK, GDN_DK), BF16),
            jax.ShapeDtypeStruct((nb, nc, GDN_HEADS, 2 * CHUNK, GDN_DK), BF16),
            jax.ShapeDtypeStruct((nb, nc, GDN_HEADS, CHUNK, GDN_DV), F32),
            jax.ShapeDtypeStruct((nb, nc, GDN_HEADS, LANES), F32),
        ],
        scratch_shapes=[pltpu.VMEM((3, tr + SUBLANES, GDN_QK), F32)],
        compiler_params=pltpu.CompilerParams(
            dimension_semantics=("parallel", "arbitrary"), vmem_limit_bytes=VMEM_LIMIT),
        name="gdn_prep",
    )(proj, proj, proj, proj, halo, cw, arow, dtrow, _inverse_masks())


def _gdn_scan_kernel(lhs1_ref, lhs2_ref, u_ref, ge_ref, z_ref, s0_ref, nb_ref, yb_ref, sout_ref, s_ref, *, tr):
    c = pl.program_id(1)

    @pl.when(c == 0)
    def _():
        s_ref[...] = s0_ref[0]

    for h in range(GDN_HEADS):
        s = s_ref[h]
        m1 = _dot(lhs1_ref[0, 0, h], s.astype(BF16))
        w = u_ref[0, 0, h] - m1[:CHUNK]
        m2 = _dot(lhs2_ref[0, 0, h], w.astype(BF16))
        o = (m1[CHUNK:] + m2[:CHUNK])[:tr]
        s_ref[h] = ge_ref[0, 0, h:h + 1, :] * s + m2[CHUNK:]
        sl = slice(h * GDN_DV, (h + 1) * GDN_DV)
        yb_ref[:, sl] = (_rms(o, nb_ref[...]) * _silu(z_ref[:, sl])).astype(BF16)

    @pl.when(c == pl.num_programs(1) - 1)
    def _():
        sout_ref[0] = s_ref[...]


def _gdn_scan(lhs1, lhs2, u, ge, proj, s0, nbw, *, nb, tr, nc, row_block0):
    blk5 = lambda b, c: (b, c, 0, 0, 0)
    return pl.pallas_call(
        functools.partial(_gdn_scan_kernel, tr=tr),
        grid=(nb, nc),
        in_specs=[
            pl.BlockSpec((1, 1, GDN_HEADS, 2 * CHUNK, GDN_DK), blk5),
            pl.BlockSpec((1, 1, GDN_HEADS, 2 * CHUNK, GDN_DK), blk5),
            pl.BlockSpec((1, 1, GDN_HEADS, CHUNK, GDN_DV), blk5),
            pl.BlockSpec((1, 1, GDN_HEADS, LANES), lambda b, c: (b, c, 0, 0)),
            pl.BlockSpec((tr, GDN_VW), lambda b, c: (row_block0 + b * nc + c, COL_Z)),
            pl.BlockSpec((1, GDN_HEADS, GDN_DK, GDN_DV), lambda b, c: (b, 0, 0, 0)),
            pl.BlockSpec((1, GDN_DV), lambda b, c: (0, 0)),
        ],
        out_specs=[
            pl.BlockSpec((tr, GDN_VW), lambda b, c: (b * nc + c, 0)),
            pl.BlockSpec((1, GDN_HEADS, GDN_DK, GDN_DV), lambda b, c: (b, 0, 0, 0)),
        ],
        out_shape=[
            jax.ShapeDtypeStruct((nb * nc * tr, GDN_VW), BF16),
            jax.ShapeDtypeStruct((nb, GDN_HEADS, GDN_DK, GDN_DV), F32),
        ],
        scratch_shapes=[pltpu.VMEM((GDN_HEADS, GDN_DK, GDN_DV), F32)],
        compiler_params=pltpu.CompilerParams(
            dimension_semantics=("parallel", "arbitrary"), vmem_limit_bytes=VMEM_LIMIT),
        name="gdn_scan",
    )(lhs1, lhs2, u, ge, proj, s0, nbw)


def _halo(hist):
    return jnp.pad(hist, ((0, 0), (SUBLANES - (CONV_W - 1), 0), (0, 0)))


def _taps(w):
    return jnp.pad(w, ((0, SUBLANES - CONV_W), (0, 0)))


def _gate_weights(rg_w, ig_w):
    eye = jnp.eye(4, dtype=F32)

    def bd(w):
        return jnp.einsum('qnij,nm->qnimj', w.reshape(4, 4, LRU_BLOCK, LRU_BLOCK), eye).reshape(
            4, 4 * LRU_BLOCK, 4 * LRU_BLOCK)

    return jnp.concatenate([bd(rg_w), bd(ig_w)], axis=2).astype(BF16)


def _lane_row(vals, lane0):
    return jnp.zeros((1, LANES), F32).at[0, lane0:lane0 + vals.shape[0]].set(vals)


def _mixer(x, l, st, p):
    state_conv_a, state_lru, state_conv_b, state_delta = st
    w_in = jnp.pad(p['w_in'][l].astype(BF16), ((0, 0), (0, N_IN_PAD - N_IN)))
    proj = _inproj(x, p['mix_norm'][l][None], w_in)

    cw_a, cb_a = _taps(p['conv_a_w'][l]), p['conv_a_b'][l][None]
    wg = _gate_weights(p['rg_w'][l], p['ig_w'][l])
    lru_args = (cw_a, cb_a, wg, p['rg_b'][l][None], p['ig_b'][l][None], p['lru_lambda'][l][None],
                p['norm_a'][l][None])
    cw_b = _taps(p['conv_b_w'][l])
    arow = _lane_row(p['a_log'][l], LANE_G)
    dtrow = _lane_row(p['dt_bias'][l], LANE_G)
    nbw = p['norm_b'][l][None]

    nt = TP // TT_LRU
    ya_p, lru_p = _lru(proj, jnp.zeros((BATCH, SUBLANES, LRU_WIDTH), F32), jnp.zeros((BATCH, 1, LRU_WIDTH), F32),
                       *lru_args, nb=BATCH, tt=TT_LRU, nt=nt, row_block0=0,
                       last_tile=(VALID_P - 1) // TT_LRU, last_row=(VALID_P - 1) % TT_LRU)
    ncp = TP // CHUNK
    pre = _gdn_prep(proj, jnp.zeros((BATCH, SUBLANES, GDN_QKV), F32), cw_b, arow, dtrow,
                    nb=BATCH, tr=CHUNK, nc=ncp, row_block0=0, valid=VALID_P)
    yb_p, dl_p = _gdn_scan(*pre, proj, jnp.zeros((BATCH, GDN_HEADS, GDN_DK, GDN_DV), F32), nbw,
                           nb=BATCH, tr=CHUNK, nc=ncp, row_block0=0)

    rb0 = ROWS_P // DEC_SEQ
    ya_s, lru_s = _lru(proj, _halo(state_conv_a[l]), state_lru[l][:, None, :], *lru_args,
                       nb=DEC_BATCH, tt=DEC_SEQ, nt=1, row_block0=rb0, last_tile=0, last_row=DEC_SEQ - 1)
    pre = _gdn_prep(proj, _halo(state_conv_b[l]), cw_b, arow, dtrow,
                    nb=DEC_BATCH, tr=DEC_SEQ, nc=1, row_block0=rb0, valid=DEC_SEQ)
    yb_s, dl_s = _gdn_scan(*pre, proj, state_delta[l], nbw, nb=DEC_BATCH, tr=DEC_SEQ, nc=1, row_block0=rb0)

    ya = jnp.concatenate([ya_p, ya_s], axis=0)
    yb = jnp.concatenate([yb_p, yb_s], axis=0)
    x = _outproj(x, ya, yb, p['w_out'][l].astype(BF16))

    pp = jnp.stack([proj[b * TP + VALID_P - 3:b * TP + VALID_P] for b in range(BATCH)], axis=0)
    ps = proj[ROWS_P:].reshape(DEC_BATCH, DEC_SEQ, N_IN_PAD)[:, DEC_SEQ - 3:]
    o2 = 2 * LRU_WIDTH
    new_p = (pp[..., :LRU_WIDTH], lru_p[:, 0], pp[..., o2:o2 + GDN_QKV], dl_p)
    new_s = (ps[..., :LRU_WIDTH], lru_s[:, 0], ps[..., o2:o2 + GDN_QKV], dl_s)
    return x, new_p, new_s


def kernel(x_prompt, x_sample, state_conv_a, state_lru, state_conv_b, state_delta, meta_tokens, ffn1_norm, ffn1_w_gate, ffn1_w_up, ffn1_w_down, mix_norm, w_in, conv_a_w, conv_a_b, rg_w, rg_b, ig_w, ig_b, lru_lambda, norm_a, conv_b_w, a_log, dt_bias, norm_b, w_out, ffn2_norm, ffn2_w_gate, ffn2_w_up, ffn2_w_down, final_norm):
    p = dict(mix_norm=mix_norm, w_in=w_in, conv_a_w=conv_a_w, conv_a_b=conv_a_b, rg_w=rg_w, rg_b=rg_b,
             ig_w=ig_w, ig_b=ig_b, lru_lambda=lru_lambda, norm_a=norm_a, conv_b_w=conv_b_w, a_log=a_log,
             dt_bias=dt_bias, norm_b=norm_b, w_out=w_out)
    meta = jnp.broadcast_to(meta_tokens[None], (BATCH, N_META, D_MODEL))
    pad = jnp.zeros((BATCH, TP - VALID_P, D_MODEL), F32)
    x = jnp.concatenate([jnp.concatenate([meta, x_prompt, pad], axis=1).reshape(ROWS_P, D_MODEL),
                         x_sample.reshape(ROWS_S, D_MODEL)], axis=0)
    st = (state_conv_a, state_lru, state_conv_b, state_delta)
    fw = final_norm[None]
    news_p, news_s = [], []
    for l in range(DEPTH):
        x = _ffn(x, ffn1_norm[l][None], ffn1_w_gate[l].astype(BF16), ffn1_w_up[l].astype(BF16),
                 ffn1_w_down[l].astype(BF16), fw, False)
        x, new_p, new_s = _mixer(x, l, st, p)
        x = _ffn(x, ffn2_norm[l][None], ffn2_w_gate[l].astype(BF16), ffn2_w_up[l].astype(BF16),
                 ffn2_w_down[l].astype(BF16), fw, l == DEPTH - 1)
        news_p.append(new_p)
        news_s.append(new_s)
    y_p = x[:ROWS_P].reshape(BATCH, TP, D_MODEL)[:, N_META:VALID_P]
    y_s = x[ROWS_P:].reshape(DEC_BATCH, DEC_SEQ, D_MODEL)
    stack = lambda news, i: jnp.stack([n[i] for n in news], axis=0)
    return (y_p, y_s,
            stack(news_p, 0), stack(news_p, 1), stack(news_p, 2), stack(news_p, 3),
            stack(news_s, 0), stack(news_s, 1), stack(news_s, 2), stack(news_s, 3))
```

```python
import functools

import jax
import jax.numpy as jnp
from jax import lax
from jax.experimental import pallas as pl
from jax.experimental.pallas import tpu as pltpu

F32 = jnp.float32
BF16 = jnp.bfloat16

D_MODEL = 2048
BATCH = 2
SEQ = 8192
DEPTH = 2
DEC_BATCH = 16
DEC_SEQ = 16
N_META = 16
LRU_WIDTH = 1024
LRU_BLOCKS = 16
LRU_BLOCK = 64
LRU_C = 8.0
CONV_W = 4
GDN_HEADS = 8
GDN_DK = 128
GDN_DV = 128
GDN_QK = 1024
GDN_VW = 1024
GDN_QKV = 3072
N_IN = 6160
D_FF = 5632
EPS = 1e-6

LANES = 128
SUBLANES = 8
CHUNK = 128
VALID_P = N_META + SEQ
TP = 8320
ROWS_P = BATCH * TP
ROWS_S = DEC_BATCH * DEC_SEQ
ROWS = ROWS_P + ROWS_S
N_IN_PAD = 6400
COL_XA, COL_GA, COL_Q, COL_K, COL_V, COL_Z = 0, 1, 2, 3, 4, 5
COL_TAIL = 48
LANE_BETA = 0
LANE_G = 8

TM = 768
TF = 512
TN = 1280
TT_LRU = 640
VMEM_LIMIT = 56 * 1024 * 1024


def _rms(x, w):
    return x * lax.rsqrt(jnp.mean(x * x, axis=-1, keepdims=True) + EPS) * w


def _silu(x):
    return x * jax.nn.sigmoid(x)


def _softplus(x):
    return jnp.maximum(x, 0.0) + jnp.log1p(jnp.exp(-jnp.abs(x)))


def _dot(a, b):
    return jnp.dot(a, b, preferred_element_type=F32)


def _ffn_kernel(x_ref, nw_ref, wg_ref, wu_ref, wd_ref, fw_ref, o_ref, h_ref, *, final_norm):
    j = pl.program_id(1)

    @pl.when(j == 0)
    def _():
        x = x_ref[...]
        h_ref[...] = _rms(x, nw_ref[...]).astype(BF16)
        o_ref[...] = x

    h = h_ref[...]
    g = _dot(h, wg_ref[...])
    u = _dot(h, wu_ref[...])
    a = (_silu(g) * u * 0.5).astype(BF16)
    o_ref[...] += _dot(a, wd_ref[...])

    if final_norm:
        @pl.when(j == pl.num_programs(1) - 1)
        def _():
            o_ref[...] = _rms(o_ref[...], fw_ref[...])


def _ffn(x, nw, wg, wu, wd, fw, final_norm):
    return pl.pallas_call(
        functools.partial(_ffn_kernel, final_norm=final_norm),
        grid=(ROWS // TM, D_FF // TF),
        in_specs=[
            pl.BlockSpec((TM, D_MODEL), lambda i, j: (i, 0)),
            pl.BlockSpec((1, D_MODEL), lambda i, j: (0, 0)),
            pl.BlockSpec((D_MODEL, TF), lambda i, j: (0, j)),
            pl.BlockSpec((D_MODEL, TF), lambda i, j: (0, j)),
            pl.BlockSpec((TF, D_MODEL), lambda i, j: (j, 0)),
            pl.BlockSpec((1, D_MODEL), lambda i, j: (0, 0)),
        ],
        out_specs=pl.BlockSpec((TM, D_MODEL), lambda i, j: (i, 0)),
        out_shape=jax.ShapeDtypeStruct((ROWS, D_MODEL), F32),
        scratch_shapes=[pltpu.VMEM((TM, D_MODEL), BF16)],
        compiler_params=pltpu.CompilerParams(
            dimension_semantics=("parallel", "arbitrary"), vmem_limit_bytes=VMEM_LIMIT),
        name="ffn",
    )(x, nw, wg, wu, wd, fw)


def _inproj_kernel(x_ref, nw_ref, w_ref, o_ref, h_ref):
    @pl.when(pl.program_id(1) == 0)
    def _():
        h_ref[...] = _rms(x_ref[...], nw_ref[...]).astype(BF16)

    o_ref[...] = _dot(h_ref[...], w_ref[...])


def _inproj(x, nw, w):
    return pl.pallas_call(
        _inproj_kernel,
        grid=(ROWS // TM, N_IN_PAD // TN),
        in_specs=[
            pl.BlockSpec((TM, D_MODEL), lambda i, j: (i, 0)),
            pl.BlockSpec((1, D_MODEL), lambda i, j: (0, 0)),
            pl.BlockSpec((D_MODEL, TN), lambda i, j: (0, j)),
        ],
        out_specs=pl.BlockSpec((TM, TN), lambda i, j: (i, j)),
        out_shape=jax.ShapeDtypeStruct((ROWS, N_IN_PAD), F32),
        scratch_shapes=[pltpu.VMEM((TM, D_MODEL), BF16)],
        compiler_params=pltpu.CompilerParams(
            dimension_semantics=("parallel", "arbitrary"), vmem_limit_bytes=VMEM_LIMIT),
        name="inproj",
    )(x, nw, w)


def _outproj_kernel(x_ref, ya_ref, yb_ref, w_ref, o_ref):
    o_ref[...] = (x_ref[...] + _dot(ya_ref[...], w_ref[0:LRU_WIDTH, :])
                  + _dot(yb_ref[...], w_ref[LRU_WIDTH:, :]))


def _outproj(x, ya, yb, w):
    return pl.pallas_call(
        _outproj_kernel,
        grid=(ROWS // TM,),
        in_specs=[
            pl.BlockSpec((TM, D_MODEL), lambda i: (i, 0)),
            pl.BlockSpec((TM, LRU_WIDTH), lambda i: (i, 0)),
            pl.BlockSpec((TM, GDN_VW), lambda i: (i, 0)),
            pl.BlockSpec((D_MODEL, D_MODEL), lambda i: (0, 0)),
        ],
        out_specs=pl.BlockSpec((TM, D_MODEL), lambda i: (i, 0)),
        out_shape=jax.ShapeDtypeStruct((ROWS, D_MODEL), F32),
        compiler_params=pltpu.CompilerParams(
            dimension_semantics=("parallel",), vmem_limit_bytes=VMEM_LIMIT),
        name="outproj",
    )(x, ya, yb, w)


def _conv_from_scratch(xe_ref, cw, tt):
    y = cw[0:1] * xe_ref[5:5 + tt, :]
    for i in range(1, CONV_W):
        y = y + cw[i:i + 1] * xe_ref[5 + i:5 + i + tt, :]
    return y


def _lru_kernel(xa_ref, ga_ref, halo_ref, h0_ref, cw_ref, cb_ref, wg_ref, rgb_ref, igb_ref, lam_ref, na_ref,
                ya_ref, last_ref, xe_ref, a_ref, b_ref, hc_ref, *, tt, last_tile, last_row):
    t = pl.program_id(1)

    @pl.when(t == 0)
    def _():
        xe_ref[0:SUBLANES, :] = halo_ref[0]
        hc_ref[...] = h0_ref[0]

    @pl.when(t > 0)
    def _():
        xe_ref[0:SUBLANES, :] = xe_ref[tt:tt + SUBLANES, :]

    xe_ref[SUBLANES:tt + SUBLANES, :] = xa_ref[...]
    xc = _conv_from_scratch(xe_ref, cw_ref[...], tt) + cb_ref[...]

    gw = 4 * LRU_BLOCK
    r_parts, i_parts = [], []
    for q in range(LRU_WIDTH // gw):
        gg = _dot(xc[:, q * gw:(q + 1) * gw].astype(BF16), wg_ref[q])
        r_parts.append(gg[:, :gw])
        i_parts.append(gg[:, gw:])
    r = jax.nn.sigmoid(jnp.concatenate(r_parts, axis=1) + rgb_ref[...])
    ig = jax.nn.sigmoid(jnp.concatenate(i_parts, axis=1) + igb_ref[...])
    log_a = (-LRU_C) * r * _softplus(-lam_ref[...])
    a = jnp.exp(log_a)
    a_ref[...] = a
    b_ref[...] = jnp.sqrt(-jnp.tanh(log_a) * (a * a + 1.0)) * ig * xc

    row = lax.broadcasted_iota(jnp.int32, (SUBLANES, LRU_WIDTH), 0)

    def group(gi, hc):
        off = pl.multiple_of(gi * SUBLANES, SUBLANES)
        a8 = a_ref[pl.ds(off, SUBLANES), :]
        b8 = b_ref[pl.ds(off, SUBLANES), :]
        for k in (1, 2, 4):
            keep = row >= k
            a_prev = jnp.where(keep, pltpu.roll(a8, k, 0), 1.0)
            b_prev = jnp.where(keep, pltpu.roll(b8, k, 0), 0.0)
            b8 = a8 * b_prev + b8
            a8 = a8 * a_prev
        h8 = a8 * hc + b8
        a_ref[pl.ds(off, SUBLANES), :] = h8
        return h8[SUBLANES - 1:SUBLANES, :]

    hc_ref[...] = lax.fori_loop(0, tt // SUBLANES, group, hc_ref[...])

    h = a_ref[...]
    ya_ref[...] = (_rms(h, na_ref[...]) * jax.nn.gelu(ga_ref[...], approximate=True)).astype(BF16)

    @pl.when(t == last_tile)
    def _():
        last_ref[0] = a_ref[last_row:last_row + 1, :]


def _lru(proj, halo, h0, cw, cb, wg, rgb, igb, lam, na, *, nb, tt, nt, row_block0, last_tile, last_row):
    row_map = lambda col: (lambda b, t: (row_block0 + b * nt + t, col))
    const2 = lambda b, t: (0, 0)
    return pl.pallas_call(
        functools.partial(_lru_kernel, tt=tt, last_tile=last_tile, last_row=last_row),
        grid=(nb, nt),
        in_specs=[
            pl.BlockSpec((tt, LRU_WIDTH), row_map(COL_XA)),
            pl.BlockSpec((tt, LRU_WIDTH), row_map(COL_GA)),
            pl.BlockSpec((1, SUBLANES, LRU_WIDTH), lambda b, t: (b, 0, 0)),
            pl.BlockSpec((1, 1, LRU_WIDTH), lambda b, t: (b, 0, 0)),
            pl.BlockSpec((SUBLANES, LRU_WIDTH), const2),
            pl.BlockSpec((1, LRU_WIDTH), const2),
            pl.BlockSpec((4, 4 * LRU_BLOCK, 8 * LRU_BLOCK), lambda b, t: (0, 0, 0)),
            pl.BlockSpec((1, LRU_WIDTH), const2),
            pl.BlockSpec((1, LRU_WIDTH), const2),
            pl.BlockSpec((1, LRU_WIDTH), const2),
            pl.BlockSpec((1, LRU_WIDTH), const2),
        ],
        out_specs=[
            pl.BlockSpec((tt, LRU_WIDTH), lambda b, t: (b * nt + t, 0)),
            pl.BlockSpec((1, 1, LRU_WIDTH), lambda b, t: (b, 0, 0)),
        ],
        out_shape=[
            jax.ShapeDtypeStruct((nb * nt * tt, LRU_WIDTH), BF16),
            jax.ShapeDtypeStruct((nb, 1, LRU_WIDTH), F32),
        ],
        scratch_shapes=[
            pltpu.VMEM((tt + SUBLANES, LRU_WIDTH), F32),
            pltpu.VMEM((tt, LRU_WIDTH), F32),
            pltpu.VMEM((tt, LRU_WIDTH), F32),
            pltpu.VMEM((1, LRU_WIDTH), F32),
        ],
        compiler_params=pltpu.CompilerParams(
            dimension_semantics=("parallel", "arbitrary"), vmem_limit_bytes=VMEM_LIMIT),
        name="rglru",
    )(proj, proj, halo, h0, cw, cb, wg, rgb, igb, lam, na)


def _split_bf16(x):
    hi = x.astype(BF16)
    lo = (x - hi.astype(F32)).astype(BF16)
    return hi, lo


def _pdot(x, y):
    (xh, xl), (yh, yl) = x, y
    lhs = jnp.concatenate([xh, xl, xh], axis=1)
    rhs = jnp.concatenate([yh, yh, yl], axis=0)
    return _dot(lhs, rhs)


N_LEVELS = 4


def _inverse_masks():
    ri = lax.broadcasted_iota(jnp.int32, (CHUNK, CHUNK), 0)
    ci = lax.broadcasted_iota(jnp.int32, (CHUNK, CHUNK), 1)
    masks = [ri == ci, (ri // SUBLANES) == (ci // SUBLANES)]
    m = SUBLANES
    while m < CHUNK:
        masks.append(((ri // (2 * m)) == (ci // (2 * m))) & ((ri // m) != (ci // m)) & (ri > ci))
        m *= 2
    return jnp.stack(masks, axis=0).astype(BF16)


def _unit_lower_inverses(a_list, mask_ref):
    eye = mask_ref[0]
    blk = mask_ref[1]
    a_s = [_split_bf16(a) for a in a_list]
    a0_s = [(ah * blk, al * blk) for ah, al in a_s]
    p2_s = [_split_bf16(_pdot(a0, a0)) for a0 in a0_s]
    p4_s = [_split_bf16(_pdot(p2, p2)) for p2 in p2_s]
    t1_s = [_split_bf16(_pdot((eye - a0h, -a0l), (eye + p2h, p2l))) for (a0h, a0l), (p2h, p2l) in zip(a0_s, p2_s)]
    t_list = [_pdot(t1, (eye + p4h, p4l)) for t1, (p4h, p4l) in zip(t1_s, p4_s)]
    for lvl in range(N_LEVELS):
        sub = mask_ref[2 + lvl]
        t_s = [_split_bf16(t) for t in t_list]
        et_s = [_split_bf16(_pdot((ah * sub, al * sub), ts)) for (ah, al), ts in zip(a_s, t_s)]
        t_list = [t - _pdot(ts, et) for t, ts, et in zip(t_list, t_s, et_s)]
    return t_list


def _pad_rows(x, rows):
    if x.shape[0] == rows:
        return x
    return jnp.concatenate([x, jnp.zeros((rows - x.shape[0], x.shape[1]), x.dtype)], axis=0)


def _gdn_recurrence(lhs1_ref, lhs2_ref, u_ref, ge_ref, z_ref, nb_ref, yb_ref, s_ref, tr):
    for h in range(GDN_HEADS):
        s = s_ref[h]
        m1 = _dot(lhs1_ref[h], s.astype(BF16))
        w = u_ref[h] - m1[:CHUNK]
        m2 = _dot(lhs2_ref[h], w.astype(BF16))
        o = (m1[CHUNK:] + m2[:CHUNK])[:tr]
        s_ref[h] = ge_ref[h:h + 1, :] * s + m2[CHUNK:]
        sl = slice(h * GDN_DV, (h + 1) * GDN_DV)
        yb_ref[:, sl] = (_rms(o, nb_ref[...]) * _silu(z_ref[:, sl])).astype(BF16)


def _gdn_kernel(q_ref, k_ref, v_ref, tail_ref, z_ref, halo_ref, s0_ref, cw_ref, arow_ref, dtrow_ref, mask_ref,
                nb_ref, yb_ref, sout_ref, xe_ref, s_ref, lhs1_ref, lhs2_ref, u_ref, ge_ref,
                *, tr, nc, valid, overlap):
    c = pl.program_id(1)
    srcs = (q_ref, k_ref, v_ref)

    @pl.when(c == 0)
    def _():
        for i in range(3):
            xe_ref[i, 0:SUBLANES, :] = halo_ref[0, :, i * GDN_QK:(i + 1) * GDN_QK]
        s_ref[...] = s0_ref[0]
        lhs1_ref[...] = jnp.zeros(lhs1_ref.shape, BF16)
        lhs2_ref[...] = jnp.zeros(lhs2_ref.shape, BF16)
        u_ref[...] = jnp.zeros(u_ref.shape, F32)
        ge_ref[...] = jnp.ones(ge_ref.shape, F32)

    @pl.when(c > 0)
    def _():
        for i in range(3):
            xe_ref[i, 0:SUBLANES, :] = xe_ref[i, tr:tr + SUBLANES, :]

    recurrence = functools.partial(_gdn_recurrence, lhs1_ref, lhs2_ref, u_ref, ge_ref, z_ref, nb_ref, yb_ref,
                                   s_ref, tr)
    prepare = functools.partial(_gdn_prepare, srcs, tail_ref, cw_ref, arow_ref, dtrow_ref, mask_ref,
                                lhs1_ref, lhs2_ref, u_ref, ge_ref, xe_ref, tr, valid - jnp.minimum(c, nc - 1) * tr)
    if overlap:
        recurrence()
        prepare()
    else:
        pl.when(c > 0)(recurrence)
        pl.when(c < nc)(prepare)

    @pl.when(c == nc)
    def _():
        sout_ref[0] = s_ref[...]


def _gdn_prepare(srcs, tail_ref, cw_ref, arow_ref, dtrow_ref, mask_ref, lhs1_ref, lhs2_ref, u_ref, ge_ref, xe_ref,
                 tr, rows_left):
    qkv = []
    for i in range(3):
        xe_ref[i, SUBLANES:tr + SUBLANES, :] = srcs[i][...]
        y = _conv_from_scratch(xe_ref.at[i], cw_ref[:, i * GDN_QK:(i + 1) * GDN_QK], tr)
        qkv.append(_pad_rows(_silu(y), CHUNK))
    q_all, k_all, v_all = qkv

    tail = tail_ref[...]
    live =(lax.broadcasted_iota(jnp.int32, (tr, LANES), 0) < rows_left).astype(F32)
    beta = _pad_rows(jax.nn.sigmoid(tail) * live, CHUNK)
    g = _pad_rows(-jnp.exp(arow_ref[...]) * _softplus(tail + dtrow_ref[...]) * live, CHUNK)

    ri = lax.broadcasted_iota(jnp.int32, (CHUNK, CHUNK), 0)
    ci = lax.broadcasted_iota(jnp.int32, (CHUNK, CHUNK), 1)
    incl = ri >= ci
    strict = ri > ci
    gc = jnp.dot(incl.astype(F32), g, precision=lax.Precision.HIGHEST, preferred_element_type=F32)
    gc_t = gc.T
    ge_ref[...] = jnp.broadcast_to(
        jnp.exp(gc_t[LANE_G:LANE_G + GDN_HEADS, CHUNK - 1:CHUNK]), (GDN_HEADS, LANES))

    a_list, rhs_list = [], []
    for h in range(GDN_HEADS):
        sl = slice(h * GDN_DK, (h + 1) * GDN_DK)
        qh, kh, vh = q_all[:, sl], k_all[:, sl], v_all[:, sl]
        qh = qh * lax.rsqrt(jnp.sum(qh * qh, axis=-1, keepdims=True) + EPS) * (GDN_DK ** -0.5)
        kh = kh * lax.rsqrt(jnp.sum(kh * kh, axis=-1, keepdims=True) + EPS)
        gcol = gc[:, LANE_G + h:LANE_G + h + 1]
        grow = gc_t[LANE_G + h:LANE_G + h + 1, :]
        bcol = beta[:, LANE_BETA + h:LANE_BETA + h + 1]
        decay = jnp.where(incl, jnp.exp(jnp.where(incl, gcol - grow, 0.0)), 0.0)
        kb = kh.astype(BF16)
        qkk = lax.dot_general(jnp.concatenate([qh.astype(BF16), kb], axis=0), kb,
                              (((1,), (1,)), ((), ())), preferred_element_type=F32)
        a_list.append(jnp.where(strict, bcol * decay * qkk[CHUNK:], 0.0))
        egc = jnp.exp(gcol)
        rhs_list.append(jnp.concatenate([bcol * vh, (bcol * egc) * kh], axis=1))
        lhs1_ref[h, CHUNK:, :] = (qh * egc).astype(BF16)
        k_end = kh * jnp.exp(grow[:, CHUNK - 1:CHUNK] - gcol)
        lhs2_ref[h, 0:CHUNK, :] = (qkk[:CHUNK] * decay).astype(BF16)
        lhs2_ref[h, CHUNK:, :] = k_end.T.astype(BF16)

    t_list = _unit_lower_inverses(a_list, mask_ref)
    for h in range(GDN_HEADS):
        sol = _pdot(_split_bf16(t_list[h]), _split_bf16(rhs_list[h]))
        u_ref[h] = sol[:, :GDN_DV]
        lhs1_ref[h, 0:CHUNK, :] = sol[:, GDN_DV:].astype(BF16)


def _gdn(proj, halo, s0, cw, arow, dtrow, nbw, *, nb, tr, nc, row_block0, valid, overlap):
    prep_map = lambda col: (lambda b, c: (row_block0 + b * nc + jnp.minimum(c, nc - 1), col))
    const2 = lambda b, c: (0, 0)
    return pl.pallas_call(
        functools.partial(_gdn_kernel, tr=tr, nc=nc, valid=valid, overlap=overlap),
        grid=(nb, nc + 1),
        in_specs=[
            pl.BlockSpec((tr, GDN_QK), prep_map(COL_Q)),
            pl.BlockSpec((tr, GDN_QK), prep_map(COL_K)),
            pl.BlockSpec((tr, GDN_VW), prep_map(COL_V)),
            pl.BlockSpec((tr, LANES), prep_map(COL_TAIL)),
            pl.BlockSpec((tr, GDN_VW), lambda b, c: (row_block0 + b * nc + jnp.maximum(c - 1, 0), COL_Z)),
            pl.BlockSpec((1, SUBLANES, GDN_QKV), lambda b, c: (b, 0, 0)),
            pl.BlockSpec((1, GDN_HEADS, GDN_DK, GDN_DV), lambda b, c: (b, 0, 0, 0)),
            pl.BlockSpec((SUBLANES, GDN_QKV), const2),
            pl.BlockSpec((1, LANES), const2),
            pl.BlockSpec((1, LANES), const2),
            pl.BlockSpec((2 + N_LEVELS, CHUNK, CHUNK), lambda b, c: (0, 0, 0)),
            pl.BlockSpec((1, GDN_DV), const2),
        ],
        out_specs=[
            pl.BlockSpec((tr, GDN_VW), lambda b, c: (b * nc + jnp.maximum(c - 1, 0), 0)),
            pl.BlockSpec((1, GDN_HEADS, GDN_DK, GDN_DV), lambda b, c: (b, 0, 0, 0)),
        ],
        out_shape=[
            jax.ShapeDtypeStruct((nb * nc * tr, GDN_VW), BF16),
            jax.ShapeDtypeStruct((nb, GDN_HEADS, GDN_DK, GDN_DV), F32),
        ],
        scratch_shapes=[
            pltpu.VMEM((3, tr + SUBLANES, GDN_QK), F32),
            pltpu.VMEM((GDN_HEADS, GDN_DK, GDN_DV), F32),
            pltpu.VMEM((GDN_HEADS, 2 * CHUNK, GDN_DK), BF16),
            pltpu.VMEM((GDN_HEADS, 2 * CHUNK, GDN_DK), BF16),
            pltpu.VMEM((GDN_HEADS, CHUNK, GDN_DV), F32),
            pltpu.VMEM((GDN_HEADS, LANES), F32),
        ],
        compiler_params=pltpu.CompilerParams(
            dimension_semantics=("parallel", "arbitrary"), vmem_limit_bytes=VMEM_LIMIT),
        name="gdn",
    )(proj, proj, proj, proj, proj, halo, s0, cw, arow, dtrow, _inverse_masks(), nbw)


def _halo(hist):
    return jnp.pad(hist, ((0, 0), (SUBLANES - (CONV_W - 1), 0), (0, 0)))


def _taps(w):
    return jnp.pad(w, ((0, SUBLANES - CONV_W), (0, 0)))


def _gate_weights(rg_w, ig_w):
    eye = jnp.eye(4, dtype=F32)

    def bd(w):
        return jnp.einsum('qnij,nm->qnimj', w.reshape(4, 4, LRU_BLOCK, LRU_BLOCK), eye).reshape(
            4, 4 * LRU_BLOCK, 4 * LRU_BLOCK)

    return jnp.concatenate([bd(rg_w), bd(ig_w)], axis=2).astype(BF16)


def _lane_row(vals, lane0):
    return jnp.zeros((1, LANES), F32).at[0, lane0:lane0 + vals.shape[0]].set(vals)


def _mixer(x, l, st, p):
    state_conv_a, state_lru, state_conv_b, state_delta = st
    w_in = jnp.pad(p['w_in'][l].astype(BF16), ((0, 0), (0, N_IN_PAD - N_IN)))
    proj = _inproj(x, p['mix_norm'][l][None], w_in)

    cw_a, cb_a = _taps(p['conv_a_w'][l]), p['conv_a_b'][l][None]
    wg = _gate_weights(p['rg_w'][l], p['ig_w'][l])
    lru_args = (cw_a, cb_a, wg, p['rg_b'][l][None], p['ig_b'][l][None], p['lru_lambda'][l][None],
                p['norm_a'][l][None])
    cw_b = _taps(p['conv_b_w'][l])
    arow = _lane_row(p['a_log'][l], LANE_G)
    dtrow = _lane_row(p['dt_bias'][l], LANE_G)
    nbw = p['norm_b'][l][None]

    nt = TP // TT_LRU
    ya_p, lru_p = _lru(proj, jnp.zeros((BATCH, SUBLANES, LRU_WIDTH), F32), jnp.zeros((BATCH, 1, LRU_WIDTH), F32),
                       *lru_args, nb=BATCH, tt=TT_LRU, nt=nt, row_block0=0,
                       last_tile=(VALID_P - 1) // TT_LRU, last_row=(VALID_P - 1) % TT_LRU)
    ncp = TP // CHUNK
    yb_p, dl_p = _gdn(proj, jnp.zeros((BATCH, SUBLANES, GDN_QKV), F32),
                      jnp.zeros((BATCH, GDN_HEADS, GDN_DK, GDN_DV), F32), cw_b, arow, dtrow, nbw,
                      nb=BATCH, tr=CHUNK, nc=ncp, row_block0=0, valid=VALID_P, overlap=True)

    rb0 = ROWS_P // DEC_SEQ
    ya_s, lru_s = _lru(proj, _halo(state_conv_a[l]), state_lru[l][:, None, :], *lru_args,
                       nb=DEC_BATCH, tt=DEC_SEQ, nt=1, row_block0=rb0, last_tile=0, last_row=DEC_SEQ - 1)
    yb_s, dl_s = _gdn(proj, _halo(state_conv_b[l]), state_delta[l], cw_b, arow, dtrow, nbw,
                      nb=DEC_BATCH, tr=DEC_SEQ, nc=1, row_block0=rb0, valid=DEC_SEQ, overlap=False)

    ya = jnp.concatenate([ya_p, ya_s], axis=0)
    yb = jnp.concatenate([yb_p, yb_s], axis=0)
    x = _outproj(x, ya, yb, p['w_out'][l].astype(BF16))

    pp = jnp.stack([proj[b * TP + VALID_P - 3:b * TP + VALID_P] for b in range(BATCH)], axis=0)
    ps = proj[ROWS_P:].reshape(DEC_BATCH, DEC_SEQ, N_IN_PAD)[:, DEC_SEQ - 3:]
    o2 = 2 * LRU_WIDTH
    new_p = (pp[..., :LRU_WIDTH], lru_p[:, 0], pp[..., o2:o2 + GDN_QKV], dl_p)
    new_s = (ps[..., :LRU_WIDTH], lru_s[:, 0], ps[..., o2:o2 + GDN_QKV], dl_s)
    return x, new_p, new_s


def kernel(x_prompt, x_sample, state_conv_a, state_lru, state_conv_b, state_delta, meta_tokens, ffn1_norm, ffn1_w_gate, ffn1_w_up, ffn1_w_down, mix_norm, w_in, conv_a_w, conv_a_b, rg_w, rg_b, ig_w, ig_b, lru_lambda, norm_a, conv_b_w, a_log, dt_bias, norm_b, w_out, ffn2_norm, ffn2_w_gate, ffn2_w_up, ffn2_w_down, final_norm):
    p = dict(mix_norm=mix_norm, w_in=w_in, conv_a_w=conv_a_w, conv_a_b=conv_a_b, rg_w=rg_w, rg_b=rg_b,
             ig_w=ig_w, ig_b=ig_b, lru_lambda=lru_lambda, norm_a=norm_a, conv_b_w=conv_b_w, a_log=a_log,
             dt_bias=dt_bias, norm_b=norm_b, w_out=w_out)
    pad = jnp.zeros((TP - VALID_P, D_MODEL), F32)
    parts = []
    for b in range(BATCH):
        parts += [meta_tokens, x_prompt[b], pad]
    x = jnp.concatenate(parts + [x_sample.reshape(ROWS_S, D_MODEL)], axis=0)
    st = (state_conv_a, state_lru, state_conv_b, state_delta)
    fw = final_norm[None]
    news_p, news_s = [], []
    for l in range(DEPTH):
        x = _ffn(x, ffn1_norm[l][None], ffn1_w_gate[l].astype(BF16), ffn1_w_up[l].astype(BF16),
                 ffn1_w_down[l].astype(BF16), fw, False)
        x, new_p, new_s = _mixer(x, l, st, p)
        x = _ffn(x, ffn2_norm[l][None], ffn2_w_gate[l].astype(BF16), ffn2_w_up[l].astype(BF16),
                 ffn2_w_down[l].astype(BF16), fw, l == DEPTH - 1)
        news_p.append(new_p)
        news_s.append(new_s)
    y_p = jnp.stack([x[b * TP + N_META:b * TP + VALID_P] for b in range(BATCH)], axis=0)
    y_s = x[ROWS_P:].reshape(DEC_BATCH, DEC_SEQ, D_MODEL)
    stack = lambda news, i: jnp.stack([n[i] for n in news], axis=0)
    return (y_p, y_s,
            stack(news_p, 0), stack(news_p, 1), stack(news_p, 2), stack(news_p, 3),
            stack(news_s, 0), stack(news_s, 1), stack(news_s, 2), stack(news_s, 3))
```

```python
import functools

import jax
import jax.numpy as jnp
from jax import lax
from jax.experimental import pallas as pl
from jax.experimental.pallas import tpu as pltpu

F32 = jnp.float32
BF16 = jnp.bfloat16

D_MODEL = 2048
BATCH = 2
SEQ = 8192
DEPTH = 2
DEC_BATCH = 16
DEC_SEQ = 16
N_META = 16
LRU_WIDTH = 1024
LRU_BLOCKS = 16
LRU_BLOCK = 64
LRU_C = 8.0
CONV_W = 4
GDN_HEADS = 8
GDN_DK = 128
GDN_DV = 128
GDN_QK = 1024
GDN_VW = 1024
GDN_QKV = 3072
N_IN = 6160
D_FF = 5632
EPS = 1e-6

LANES = 128
SUBLANES = 8
CHUNK = 128
VALID_P = N_META + SEQ
TP = 8320
ROWS_P = BATCH * TP
ROWS_S = DEC_BATCH * DEC_SEQ
ROWS = ROWS_P + ROWS_S
N_IN_PAD = 6400
COL_XA, COL_GA, COL_Q, COL_K, COL_V, COL_Z = 0, 1, 2, 3, 4, 5
COL_TAIL = 48
LANE_BETA = 0
LANE_G = 8

TM = 768
TF = 512
TN = 1280
TT_LRU = 640
VMEM_LIMIT = 56 * 1024 * 1024


def _rms(x, w):
    return x * lax.rsqrt(jnp.mean(x * x, axis=-1, keepdims=True) + EPS) * w


def _silu(x):
    return x * jax.nn.sigmoid(x)


def _softplus(x):
    return jnp.maximum(x, 0.0) + jnp.log1p(jnp.exp(-jnp.abs(x)))


def _dot(a, b):
    return jnp.dot(a, b, preferred_element_type=F32)


def _ffn_kernel(x_ref, nw_ref, wg_ref, wu_ref, wd_ref, fw_ref, o_ref, h_ref, *, final_norm):
    j = pl.program_id(1)

    @pl.when(j == 0)
    def _():
        x = x_ref[...]
        h_ref[...] = _rms(x, nw_ref[...]).astype(BF16)
        o_ref[...] = x

    h = h_ref[...]
    g = _dot(h, wg_ref[...])
    u = _dot(h, wu_ref[...])
    a = (_silu(g) * u * 0.5).astype(BF16)
    o_ref[...] += _dot(a, wd_ref[...])

    if final_norm:
        @pl.when(j == pl.num_programs(1) - 1)
        def _():
            o_ref[...] = _rms(o_ref[...], fw_ref[...])


def _ffn(x, nw, wg, wu, wd, fw, l, final_norm):
    return pl.pallas_call(
        functools.partial(_ffn_kernel, final_norm=final_norm),
        grid=(ROWS // TM, D_FF // TF),
        in_specs=[
            pl.BlockSpec((TM, D_MODEL), lambda i, j: (i, 0)),
            pl.BlockSpec((1, D_MODEL), lambda i, j: (0, 0)),
            pl.BlockSpec((None, D_MODEL, TF), lambda i, j: (l, 0, j)),
            pl.BlockSpec((None, D_MODEL, TF), lambda i, j: (l, 0, j)),
            pl.BlockSpec((None, TF, D_MODEL), lambda i, j: (l, j, 0)),
            pl.BlockSpec((1, D_MODEL), lambda i, j: (0, 0)),
        ],
        out_specs=pl.BlockSpec((TM, D_MODEL), lambda i, j: (i, 0)),
        out_shape=jax.ShapeDtypeStruct((ROWS, D_MODEL), F32),
        scratch_shapes=[pltpu.VMEM((TM, D_MODEL), BF16)],
        compiler_params=pltpu.CompilerParams(
            dimension_semantics=("parallel", "arbitrary"), vmem_limit_bytes=VMEM_LIMIT),
        name="ffn",
    )(x, nw, wg, wu, wd, fw)


def _inproj_kernel(x_ref, nw_ref, w_ref, o_ref, h_ref):
    @pl.when(pl.program_id(1) == 0)
    def _():
        h_ref[...] = _rms(x_ref[...], nw_ref[...]).astype(BF16)

    o_ref[...] = _dot(h_ref[...], w_ref[...])


def _inproj(x, nw, w, l):
    return pl.pallas_call(
        _inproj_kernel,
        grid=(ROWS // TM, N_IN_PAD // TN),
        in_specs=[
            pl.BlockSpec((TM, D_MODEL), lambda i, j: (i, 0)),
            pl.BlockSpec((1, D_MODEL), lambda i, j: (0, 0)),
            pl.BlockSpec((None, D_MODEL, TN), lambda i, j: (l, 0, j)),
        ],
        out_specs=pl.BlockSpec((TM, TN), lambda i, j: (i, j)),
        out_shape=jax.ShapeDtypeStruct((ROWS, N_IN_PAD), F32),
        scratch_shapes=[pltpu.VMEM((TM, D_MODEL), BF16)],
        compiler_params=pltpu.CompilerParams(
            dimension_semantics=("parallel", "arbitrary"), vmem_limit_bytes=VMEM_LIMIT),
        name="inproj",
    )(x, nw, w)


def _outproj_kernel(x_ref, ya_ref, yb_ref, w_ref, o_ref):
    o_ref[...] = (x_ref[...] + _dot(ya_ref[...], w_ref[0:LRU_WIDTH, :])
                  + _dot(yb_ref[...], w_ref[LRU_WIDTH:, :]))


def _outproj(x, ya, yb, w, l):
    return pl.pallas_call(
        _outproj_kernel,
        grid=(ROWS // TM,),
        in_specs=[
            pl.BlockSpec((TM, D_MODEL), lambda i: (i, 0)),
            pl.BlockSpec((TM, LRU_WIDTH), lambda i: (i, 0)),
            pl.BlockSpec((TM, GDN_VW), lambda i: (i, 0)),
            pl.BlockSpec((None, D_MODEL, D_MODEL), lambda i: (l, 0, 0)),
        ],
        out_specs=pl.BlockSpec((TM, D_MODEL), lambda i: (i, 0)),
        out_shape=jax.ShapeDtypeStruct((ROWS, D_MODEL), F32),
        compiler_params=pltpu.CompilerParams(
            dimension_semantics=("parallel",), vmem_limit_bytes=VMEM_LIMIT),
        name="outproj",
    )(x, ya, yb, w)


def _conv_from_scratch(xe_ref, cw, tt):
    width = xe_ref.shape[-1]
    groups = tt // SUBLANES + 1
    x3 = xe_ref[...].reshape(groups, SUBLANES, width)
    row = lax.broadcasted_iota(jnp.int32, (1, SUBLANES, width), 1)
    y = cw[CONV_W - 1:CONV_W][None] * x3[1:]
    for s in range(1, CONV_W):
        rot = pltpu.roll(x3, s, 1)
        y = y + cw[CONV_W - 1 - s:CONV_W - s][None] * jnp.where(row < s, rot[:-1], rot[1:])
    return y.reshape(tt, width)


def _lru_kernel(xa_ref, ga_ref, halo_ref, h0_ref, cw_ref, cb_ref, wg_ref, rgb_ref, igb_ref, lam_ref, na_ref,
                ya_ref, last_ref, xe_ref, a_ref, b_ref, hc_ref, *, tt, last_tile, last_row):
    t = pl.program_id(1)

    @pl.when(t == 0)
    def _():
        xe_ref[0:SUBLANES, :] = halo_ref[0]
        hc_ref[...] = h0_ref[0]

    @pl.when(t > 0)
    def _():
        xe_ref[0:SUBLANES, :] = xe_ref[tt:tt + SUBLANES, :]

    xe_ref[SUBLANES:tt + SUBLANES, :] = xa_ref[...]
    xc = _conv_from_scratch(xe_ref, cw_ref[...], tt) + cb_ref[...]

    gw = 4 * LRU_BLOCK
    r_parts, i_parts = [], []
    for q in range(LRU_WIDTH // gw):
        gg = _dot(xc[:, q * gw:(q + 1) * gw].astype(BF16), wg_ref[q])
        r_parts.append(gg[:, :gw])
        i_parts.append(gg[:, gw:])
    r = jax.nn.sigmoid(jnp.concatenate(r_parts, axis=1) + rgb_ref[...])
    ig = jax.nn.sigmoid(jnp.concatenate(i_parts, axis=1) + igb_ref[...])
    log_a = (-LRU_C) * r * _softplus(-lam_ref[...])
    a = jnp.exp(log_a)
    a_ref[...] = a
    b_ref[...] = jnp.sqrt(-jnp.tanh(log_a) * (a * a + 1.0)) * ig * xc

    row = lax.broadcasted_iota(jnp.int32, (SUBLANES, LRU_WIDTH), 0)

    def group(gi, hc):
        off = pl.multiple_of(gi * SUBLANES, SUBLANES)
        a8 = a_ref[pl.ds(off, SUBLANES), :]
        b8 = b_ref[pl.ds(off, SUBLANES), :]
        for k in (1, 2, 4):
            keep = row >= k
            a_prev = jnp.where(keep, pltpu.roll(a8, k, 0), 1.0)
            b_prev = jnp.where(keep, pltpu.roll(b8, k, 0), 0.0)
            b8 = a8 * b_prev + b8
            a8 = a8 * a_prev
        h8 = a8 * hc + b8
        a_ref[pl.ds(off, SUBLANES), :] = h8
        return h8[SUBLANES - 1:SUBLANES, :]

    hc_ref[...] = lax.fori_loop(0, tt // SUBLANES, group, hc_ref[...])

    h = a_ref[...]
    ya_ref[...] = (_rms(h, na_ref[...]) * jax.nn.gelu(ga_ref[...], approximate=True)).astype(BF16)

    @pl.when(t == last_tile)
    def _():
        last_ref[0] = a_ref[last_row:last_row + 1, :]


def _lru(proj, halo, h0, cw, cb, wg, rgb, igb, lam, na, *, nb, tt, nt, row_block0, last_tile, last_row):
    row_map = lambda col: (lambda b, t: (row_block0 + b * nt + t, col))
    const2 = lambda b, t: (0, 0)
    return pl.pallas_call(
        functools.partial(_lru_kernel, tt=tt, last_tile=last_tile, last_row=last_row),
        grid=(nb, nt),
        in_specs=[
            pl.BlockSpec((tt, LRU_WIDTH), row_map(COL_XA)),
            pl.BlockSpec((tt, LRU_WIDTH), row_map(COL_GA)),
            pl.BlockSpec((1, SUBLANES, LRU_WIDTH), lambda b, t: (b, 0, 0)),
            pl.BlockSpec((1, 1, LRU_WIDTH), lambda b, t: (b, 0, 0)),
            pl.BlockSpec((SUBLANES, LRU_WIDTH), const2),
            pl.BlockSpec((1, LRU_WIDTH), const2),
            pl.BlockSpec((4, 4 * LRU_BLOCK, 8 * LRU_BLOCK), lambda b, t: (0, 0, 0)),
            pl.BlockSpec((1, LRU_WIDTH), const2),
            pl.BlockSpec((1, LRU_WIDTH), const2),
            pl.BlockSpec((1, LRU_WIDTH), const2),
            pl.BlockSpec((1, LRU_WIDTH), const2),
        ],
        out_specs=[
            pl.BlockSpec((tt, LRU_WIDTH), lambda b, t: (b * nt + t, 0)),
            pl.BlockSpec((1, 1, LRU_WIDTH), lambda b, t: (b, 0, 0)),
        ],
        out_shape=[
            jax.ShapeDtypeStruct((nb * nt * tt, LRU_WIDTH), BF16),
            jax.ShapeDtypeStruct((nb, 1, LRU_WIDTH), F32),
        ],
        scratch_shapes=[
            pltpu.VMEM((tt + SUBLANES, LRU_WIDTH), F32),
            pltpu.VMEM((tt, LRU_WIDTH), F32),
            pltpu.VMEM((tt, LRU_WIDTH), F32),
            pltpu.VMEM((1, LRU_WIDTH), F32),
        ],
        compiler_params=pltpu.CompilerParams(
            dimension_semantics=("parallel", "arbitrary"), vmem_limit_bytes=VMEM_LIMIT),
        name="rglru",
    )(proj, proj, halo, h0, cw, cb, wg, rgb, igb, lam, na)


def _split_bf16(x):
    hi = x.astype(BF16)
    lo = (x - hi.astype(F32)).astype(BF16)
    return hi, lo


def _pdot(x, y):
    (xh, xl), (yh, yl) = x, y
    lhs = jnp.concatenate([xh, xl, xh], axis=1)
    rhs = jnp.concatenate([yh, yh, yl], axis=0)
    return _dot(lhs, rhs)


N_LEVELS = 4
HEAD_GROUP = 8


def _inverse_masks():
    ri = lax.broadcasted_iota(jnp.int32, (CHUNK, CHUNK), 0)
    ci = lax.broadcasted_iota(jnp.int32, (CHUNK, CHUNK), 1)
    masks = [ri == ci, (ri // SUBLANES) == (ci // SUBLANES)]
    m = SUBLANES
    while m < CHUNK:
        masks.append(((ri // (2 * m)) == (ci // (2 * m))) & ((ri // m) != (ci // m)) & (ri > ci))
        m *= 2
    return jnp.stack(masks, axis=0).astype(BF16)


def _unit_lower_inverses(a_list, mask_ref):
    eye = mask_ref[0]
    blk = mask_ref[1]
    a_s = [_split_bf16(a) for a in a_list]
    a0_s = [(ah * blk, al * blk) for ah, al in a_s]
    p2_s = [_split_bf16(_pdot(a0, a0)) for a0 in a0_s]
    p4_s = [_split_bf16(_pdot(p2, p2)) for p2 in p2_s]
    t1_s = [_split_bf16(_pdot((eye - a0h, -a0l), (eye + p2h, p2l))) for (a0h, a0l), (p2h, p2l) in zip(a0_s, p2_s)]
    t_list = [_pdot(t1, (eye + p4h, p4l)) for t1, (p4h, p4l) in zip(t1_s, p4_s)]
    for lvl in range(N_LEVELS):
        sub = mask_ref[2 + lvl]
        t_s = [_split_bf16(t) for t in t_list]
        et_s = [_split_bf16(_pdot((ah * sub, al * sub), ts)) for (ah, al), ts in zip(a_s, t_s)]
        t_list = [t - _pdot(ts, et) for t, ts, et in zip(t_list, t_s, et_s)]
    return t_list


def _pad_rows(x, rows):
    if x.shape[0] == rows:
        return x
    return jnp.concatenate([x, jnp.zeros((rows - x.shape[0], x.shape[1]), x.dtype)], axis=0)


def _gdn_recurrence(lhs1_ref, lhs2_ref, u_ref, ge_ref, z_ref, nb_ref, yb_ref, s_ref, tr):
    for h in range(GDN_HEADS):
        s = s_ref[h]
        m1 = _dot(lhs1_ref[h], s.astype(BF16))
        w = u_ref[h] - m1[:CHUNK]
        m2 = _dot(lhs2_ref[h], w.astype(BF16))
        o = (m1[CHUNK:] + m2[:CHUNK])[:tr]
        s_ref[h] = ge_ref[h:h + 1, :] * s + m2[CHUNK:]
        sl = slice(h * GDN_DV, (h + 1) * GDN_DV)
        yb_ref[:, sl] = (_rms(o, nb_ref[...]) * _silu(z_ref[:, sl])).astype(BF16)


def _gdn_kernel(q_ref, k_ref, v_ref, tail_ref, z_ref, halo_ref, s0_ref, cw_ref, arow_ref, dtrow_ref, mask_ref,
                nb_ref, yb_ref, sout_ref, xe_ref, s_ref, lhs1_ref, lhs2_ref, u_ref, ge_ref,
                *, tr, nc, valid, overlap):
    c = pl.program_id(1)
    srcs = (q_ref, k_ref, v_ref)

    @pl.when(c == 0)
    def _():
        for i in range(3):
            xe_ref[i, 0:SUBLANES, :] = halo_ref[0, :, i * GDN_QK:(i + 1) * GDN_QK]
        s_ref[...] = s0_ref[0]
        lhs1_ref[...] = jnp.zeros(lhs1_ref.shape, BF16)
        lhs2_ref[...] = jnp.zeros(lhs2_ref.shape, BF16)
        u_ref[...] = jnp.zeros(u_ref.shape, F32)
        ge_ref[...] = jnp.ones(ge_ref.shape, F32)

    @pl.when(c > 0)
    def _():
        for i in range(3):
            xe_ref[i, 0:SUBLANES, :] = xe_ref[i, tr:tr + SUBLANES, :]

    recurrence = functools.partial(_gdn_recurrence, lhs1_ref, lhs2_ref, u_ref, ge_ref, z_ref, nb_ref, yb_ref,
                                   s_ref, tr)
    prepare = functools.partial(_gdn_prepare, srcs, tail_ref, cw_ref, arow_ref, dtrow_ref, mask_ref,
                                lhs1_ref, lhs2_ref, u_ref, ge_ref, xe_ref, tr, valid - jnp.minimum(c, nc - 1) * tr)
    if overlap:
        recurrence()
        prepare()
    else:
        pl.when(c > 0)(recurrence)
        pl.when(c < nc)(prepare)

    @pl.when(c == nc)
    def _():
        sout_ref[0] = s_ref[...]


def _gdn_prepare(srcs, tail_ref, cw_ref, arow_ref, dtrow_ref, mask_ref, lhs1_ref, lhs2_ref, u_ref, ge_ref, xe_ref,
                 tr, rows_left):
    qkv = []
    for i in range(3):
        xe_ref[i, SUBLANES:tr + SUBLANES, :] = srcs[i][...]
        y = _conv_from_scratch(xe_ref.at[i], cw_ref[:, i * GDN_QK:(i + 1) * GDN_QK], tr)
        qkv.append(_pad_rows(_silu(y), CHUNK))
    q_all, k_all, v_all = qkv

    tail = tail_ref[...]
    live =(lax.broadcasted_iota(jnp.int32, (tr, LANES), 0) < rows_left).astype(F32)
    beta = _pad_rows(jax.nn.sigmoid(tail) * live, CHUNK)
    g = _pad_rows(-jnp.exp(arow_ref[...]) * _softplus(tail + dtrow_ref[...]) * live, CHUNK)

    ri = lax.broadcasted_iota(jnp.int32, (CHUNK, CHUNK), 0)
    ci = lax.broadcasted_iota(jnp.int32, (CHUNK, CHUNK), 1)
    incl = ri >= ci
    strict = ri > ci
    gc = jnp.dot(incl.astype(F32), g, precision=lax.Precision.HIGHEST, preferred_element_type=F32)
    gc_t = gc.T
    ge_ref[...] = jnp.broadcast_to(
        jnp.exp(gc_t[LANE_G:LANE_G + GDN_HEADS, CHUNK - 1:CHUNK]), (GDN_HEADS, LANES))

    for h0 in range(0, GDN_HEADS, HEAD_GROUP):
        _gdn_prepare_heads(range(h0, h0 + HEAD_GROUP), q_all, k_all, v_all, gc, gc_t, beta, incl, strict,
                           mask_ref, lhs1_ref, lhs2_ref, u_ref)


def _gdn_prepare_heads(heads, q_all, k_all, v_all, gc, gc_t, beta, incl, strict, mask_ref, lhs1_ref, lhs2_ref, u_ref):
    a_list, rhs_list = [], []
    for h in heads:
        sl = slice(h * GDN_DK, (h + 1) * GDN_DK)
        qh, kh, vh = q_all[:, sl], k_all[:, sl], v_all[:, sl]
        qh = qh * lax.rsqrt(jnp.sum(qh * qh, axis=-1, keepdims=True) + EPS) * (GDN_DK ** -0.5)
        kh = kh * lax.rsqrt(jnp.sum(kh * kh, axis=-1, keepdims=True) + EPS)
        gcol = gc[:, LANE_G + h:LANE_G + h + 1]
        grow = gc_t[LANE_G + h:LANE_G + h + 1, :]
        bcol = beta[:, LANE_BETA + h:LANE_BETA + h + 1]
        decay = jnp.where(incl, jnp.exp(jnp.where(incl, gcol - grow, 0.0)), 0.0)
        kb = kh.astype(BF16)
        qkk = lax.dot_general(jnp.concatenate([qh.astype(BF16), kb], axis=0), kb,
                              (((1,), (1,)), ((), ())), preferred_element_type=F32)
        a_list.append(jnp.where(strict, bcol * decay * qkk[CHUNK:], 0.0))
        egc = jnp.exp(gcol)
        rhs_list.append(jnp.concatenate([bcol * vh, (bcol * egc) * kh], axis=1))
        lhs1_ref[h, CHUNK:, :] = (qh * egc).astype(BF16)
        k_end = kh * jnp.exp(grow[:, CHUNK - 1:CHUNK] - gcol)
        lhs2_ref[h, 0:CHUNK, :] = (qkk[:CHUNK] * decay).astype(BF16)
        lhs2_ref[h, CHUNK:, :] = k_end.T.astype(BF16)

    t_list = _unit_lower_inverses(a_list, mask_ref)
    for i, h in enumerate(heads):
        sol = _pdot(_split_bf16(t_list[i]), _split_bf16(rhs_list[i]))
        u_ref[h] = sol[:, :GDN_DV]
        lhs1_ref[h, 0:CHUNK, :] = sol[:, GDN_DV:].astype(BF16)


def _gdn(proj, halo, s0, cw, arow, dtrow, nbw, *, nb, tr, nc, row_block0, valid, overlap):
    prep_map = lambda col: (lambda b, c: (row_block0 + b * nc + jnp.minimum(c, nc - 1), col))
    const2 = lambda b, c: (0, 0)
    return pl.pallas_call(
        functools.partial(_gdn_kernel, tr=tr, nc=nc, valid=valid, overlap=overlap),
        grid=(nb, nc + 1),
        in_specs=[
            pl.BlockSpec((tr, GDN_QK), prep_map(COL_Q)),
            pl.BlockSpec((tr, GDN_QK), prep_map(COL_K)),
            pl.BlockSpec((tr, GDN_VW), prep_map(COL_V)),
            pl.BlockSpec((tr, LANES), prep_map(COL_TAIL)),
            pl.BlockSpec((tr, GDN_VW), lambda b, c: (row_block0 + b * nc + jnp.maximum(c - 1, 0), COL_Z)),
            pl.BlockSpec((1, SUBLANES, GDN_QKV), lambda b, c: (b, 0, 0)),
            pl.BlockSpec((1, GDN_HEADS, GDN_DK, GDN_DV), lambda b, c: (b, 0, 0, 0)),
            pl.BlockSpec((SUBLANES, GDN_QKV), const2),
            pl.BlockSpec((1, LANES), const2),
            pl.BlockSpec((1, LANES), const2),
            pl.BlockSpec((2 + N_LEVELS, CHUNK, CHUNK), lambda b, c: (0, 0, 0)),
            pl.BlockSpec((1, GDN_DV), const2),
        ],
        out_specs=[
            pl.BlockSpec((tr, GDN_VW), lambda b, c: (b * nc + jnp.maximum(c - 1, 0), 0)),
            pl.BlockSpec((1, GDN_HEADS, GDN_DK, GDN_DV), lambda b, c: (b, 0, 0, 0)),
        ],
        out_shape=[
            jax.ShapeDtypeStruct((nb * nc * tr, GDN_VW), BF16),
            jax.ShapeDtypeStruct((nb, GDN_HEADS, GDN_DK, GDN_DV), F32),
        ],
        scratch_shapes=[
            pltpu.VMEM((3, tr + SUBLANES, GDN_QK), F32),
            pltpu.VMEM((GDN_HEADS, GDN_DK, GDN_DV), F32),
            pltpu.VMEM((GDN_HEADS, 2 * CHUNK, GDN_DK), BF16),
            pltpu.VMEM((GDN_HEADS, 2 * CHUNK, GDN_DK), BF16),
            pltpu.VMEM((GDN_HEADS, CHUNK, GDN_DV), F32),
            pltpu.VMEM((GDN_HEADS, LANES), F32),
        ],
        compiler_params=pltpu.CompilerParams(
            dimension_semantics=("parallel", "arbitrary"), vmem_limit_bytes=VMEM_LIMIT),
        name="gdn",
    )(proj, proj, proj, proj, proj, halo, s0, cw, arow, dtrow, _inverse_masks(), nbw)


def _halo(hist):
    return jnp.pad(hist, ((0, 0), (SUBLANES - (CONV_W - 1), 0), (0, 0)))


def _taps(w):
    return jnp.pad(w, ((0, SUBLANES - CONV_W), (0, 0)))


def _gate_weights(rg_w, ig_w):
    eye = jnp.eye(4, dtype=F32)

    def bd(w):
        return jnp.einsum('qnij,nm->qnimj', w.reshape(4, 4, LRU_BLOCK, LRU_BLOCK), eye).reshape(
            4, 4 * LRU_BLOCK, 4 * LRU_BLOCK)

    return jnp.concatenate([bd(rg_w), bd(ig_w)], axis=2).astype(BF16)


def _lane_row(vals, lane0):
    return jnp.zeros((1, LANES), F32).at[0, lane0:lane0 + vals.shape[0]].set(vals)


def _mixer(x, l, st, p):
    state_conv_a, state_lru, state_conv_b, state_delta = st
    proj = _inproj(x, p['mix_norm'][l][None], p['w_in'], l)

    cw_a, cb_a = _taps(p['conv_a_w'][l]), p['conv_a_b'][l][None]
    wg = _gate_weights(p['rg_w'][l], p['ig_w'][l])
    lru_args = (cw_a, cb_a, wg, p['rg_b'][l][None], p['ig_b'][l][None], p['lru_lambda'][l][None],
                p['norm_a'][l][None])
    cw_b = _taps(p['conv_b_w'][l])
    arow = _lane_row(p['a_log'][l], LANE_G)
    dtrow = _lane_row(p['dt_bias'][l], LANE_G)
    nbw = p['norm_b'][l][None]

    nt = TP // TT_LRU
    ya_p, lru_p = _lru(proj, jnp.zeros((BATCH, SUBLANES, LRU_WIDTH), F32), jnp.zeros((BATCH, 1, LRU_WIDTH), F32),
                       *lru_args, nb=BATCH, tt=TT_LRU, nt=nt, row_block0=0,
                       last_tile=(VALID_P - 1) // TT_LRU, last_row=(VALID_P - 1) % TT_LRU)
    ncp = TP // CHUNK
    yb_p, dl_p = _gdn(proj, jnp.zeros((BATCH, SUBLANES, GDN_QKV), F32),
                      jnp.zeros((BATCH, GDN_HEADS, GDN_DK, GDN_DV), F32), cw_b, arow, dtrow, nbw,
                      nb=BATCH, tr=CHUNK, nc=ncp, row_block0=0, valid=VALID_P, overlap=True)

    rb0 = ROWS_P // DEC_SEQ
    ya_s, lru_s = _lru(proj, _halo(state_conv_a[l]), state_lru[l][:, None, :], *lru_args,
                       nb=DEC_BATCH, tt=DEC_SEQ, nt=1, row_block0=rb0, last_tile=0, last_row=DEC_SEQ - 1)
    yb_s, dl_s = _gdn(proj, _halo(state_conv_b[l]), state_delta[l], cw_b, arow, dtrow, nbw,
                      nb=DEC_BATCH, tr=DEC_SEQ, nc=1, row_block0=rb0, valid=DEC_SEQ, overlap=False)

    ya = jnp.concatenate([ya_p, ya_s], axis=0)
    yb = jnp.concatenate([yb_p, yb_s], axis=0)
    x = _outproj(x, ya, yb, p['w_out'], l)

    pp = jnp.stack([proj[b * TP + VALID_P - 3:b * TP + VALID_P] for b in range(BATCH)], axis=0)
    ps = proj[ROWS_P:].reshape(DEC_BATCH, DEC_SEQ, N_IN_PAD)[:, DEC_SEQ - 3:]
    o2 = 2 * LRU_WIDTH
    new_p = (pp[..., :LRU_WIDTH], lru_p[:, 0], pp[..., o2:o2 + GDN_QKV], dl_p)
    new_s = (ps[..., :LRU_WIDTH], lru_s[:, 0], ps[..., o2:o2 + GDN_QKV], dl_s)
    return x, new_p, new_s


def kernel(x_prompt, x_sample, state_conv_a, state_lru, state_conv_b, state_delta, meta_tokens, ffn1_norm, ffn1_w_gate, ffn1_w_up, ffn1_w_down, mix_norm, w_in, conv_a_w, conv_a_b, rg_w, rg_b, ig_w, ig_b, lru_lambda, norm_a, conv_b_w, a_log, dt_bias, norm_b, w_out, ffn2_norm, ffn2_w_gate, ffn2_w_up, ffn2_w_down, final_norm):
    w_in_b = jnp.pad(w_in.astype(BF16), ((0, 0), (0, 0), (0, N_IN_PAD - N_IN)))
    p = dict(mix_norm=mix_norm, w_in=w_in_b, conv_a_w=conv_a_w, conv_a_b=conv_a_b, rg_w=rg_w, rg_b=rg_b,
             ig_w=ig_w, ig_b=ig_b, lru_lambda=lru_lambda, norm_a=norm_a, conv_b_w=conv_b_w, a_log=a_log,
             dt_bias=dt_bias, norm_b=norm_b, w_out=w_out.astype(BF16))
    ffn1 = (ffn1_w_gate.astype(BF16), ffn1_w_up.astype(BF16), ffn1_w_down.astype(BF16))
    ffn2 = (ffn2_w_gate.astype(BF16), ffn2_w_up.astype(BF16), ffn2_w_down.astype(BF16))
    pad = jnp.zeros((TP - VALID_P, D_MODEL), F32)
    parts = []
    for b in range(BATCH):
        parts += [meta_tokens, x_prompt[b], pad]
    x = jnp.concatenate(parts + [x_sample.reshape(ROWS_S, D_MODEL)], axis=0)
    st = (state_conv_a, state_lru, state_conv_b, state_delta)
    fw = final_norm[None]
    news_p, news_s = [], []
    for l in range(DEPTH):
        x = _ffn(x, ffn1_norm[l][None], *ffn1, fw, l, False)
        x, new_p, new_s = _mixer(x, l, st, p)
        x = _ffn(x, ffn2_norm[l][None], *ffn2, fw, l, l == DEPTH - 1)
        news_p.append(new_p)
        news_s.append(new_s)
    y_p = jnp.stack([x[b * TP + N_META:b * TP + VALID_P] for b in range(BATCH)], axis=0)
    y_s = x[ROWS_P:].reshape(DEC_BATCH, DEC_SEQ, D_MODEL)
    stack = lambda news, i: jnp.stack([n[i] for n in news], axis=0)
    return (y_p, y_s,
            stack(news_p, 0), stack(news_p, 1), stack(news_p, 2), stack(news_p, 3),
            stack(news_s, 0), stack(news_s, 1), stack(news_s, 2), stack(news_s, 3))
```

```python
import functools

import jax
import jax.numpy as jnp
from jax import lax
from jax.experimental import pallas as pl
from jax.experimental.pallas import tpu as pltpu

F32 = jnp.float32
BF16 = jnp.bfloat16

D_MODEL = 2048
BATCH = 2
SEQ = 8192
DEPTH = 2
DEC_BATCH = 16
DEC_SEQ = 16
N_META = 16
LRU_WIDTH = 1024
LRU_BLOCKS = 16
LRU_BLOCK = 64
LRU_C = 8.0
CONV_W = 4
GDN_HEADS = 8
GDN_DK = 128
GDN_DV = 128
GDN_QK = 1024
GDN_VW = 1024
GDN_QKV = 3072
N_IN = 6160
D_FF = 5632
EPS = 1e-6

LANES = 128
SUBLANES = 8
CHUNK = 128
VALID_P = N_META + SEQ
TP = 8320
ROWS_P = BATCH * TP
ROWS_S = DEC_BATCH * DEC_SEQ
ROWS = ROWS_P + ROWS_S
N_IN_PAD = 6400
COL_XA, COL_GA, COL_Q, COL_K, COL_V, COL_Z = 0, 1, 2, 3, 4, 5
COL_TAIL = 48
LANE_BETA = 0
LANE_G = 8

TM = 768
TF = 512
TN = 1280
TT_LRU = 640
VMEM_LIMIT = 56 * 1024 * 1024


def _rms(x, w):
    return x * lax.rsqrt(jnp.mean(x * x, axis=-1, keepdims=True) + EPS) * w


def _silu(x):
    return x * jax.nn.sigmoid(x)


def _softplus(x):
    return jnp.maximum(x, 0.0) + jnp.log1p(jnp.exp(-jnp.abs(x)))


def _dot(a, b):
    return jnp.dot(a, b, preferred_element_type=F32)


def _ffn_kernel(x_ref, nw_ref, wg_ref, wu_ref, wd_ref, fw_ref, o_ref, h_ref, *, final_norm):
    j = pl.program_id(1)

    @pl.when(j == 0)
    def _():
        x = x_ref[...]
        h_ref[...] = _rms(x, nw_ref[...]).astype(BF16)
        o_ref[...] = x

    h = h_ref[...]
    g = _dot(h, wg_ref[...])
    u = _dot(h, wu_ref[...])
    a = (_silu(g) * u * 0.5).astype(BF16)
    o_ref[...] += _dot(a, wd_ref[...])

    if final_norm:
        @pl.when(j == pl.num_programs(1) - 1)
        def _():
            o_ref[...] = _rms(o_ref[...], fw_ref[...])


def _ffn(x, nw, wg, wu, wd, fw, l, final_norm):
    return pl.pallas_call(
        functools.partial(_ffn_kernel, final_norm=final_norm),
        grid=(ROWS // TM, D_FF // TF),
        in_specs=[
            pl.BlockSpec((TM, D_MODEL), lambda i, j: (i, 0)),
            pl.BlockSpec((1, D_MODEL), lambda i, j: (0, 0)),
            pl.BlockSpec((None, D_MODEL, TF), lambda i, j: (l, 0, j)),
            pl.BlockSpec((None, D_MODEL, TF), lambda i, j: (l, 0, j)),
            pl.BlockSpec((None, TF, D_MODEL), lambda i, j: (l, j, 0)),
            pl.BlockSpec((1, D_MODEL), lambda i, j: (0, 0)),
        ],
        out_specs=pl.BlockSpec((TM, D_MODEL), lambda i, j: (i, 0)),
        out_shape=jax.ShapeDtypeStruct((ROWS, D_MODEL), F32),
        scratch_shapes=[pltpu.VMEM((TM, D_MODEL), BF16)],
        compiler_params=pltpu.CompilerParams(
            dimension_semantics=("parallel", "arbitrary"), vmem_limit_bytes=VMEM_LIMIT),
        name="ffn",
    )(x, nw, wg, wu, wd, fw)


def _inproj_kernel(x_ref, nw_ref, w_ref, o_ref, h_ref):
    @pl.when(pl.program_id(1) == 0)
    def _():
        h_ref[...] = _rms(x_ref[...], nw_ref[...]).astype(BF16)

    o_ref[...] = _dot(h_ref[...], w_ref[...])


def _inproj(x, nw, w, l, *, tm, row_block0, rows):
    return pl.pallas_call(
        _inproj_kernel,
        grid=(rows // tm, N_IN_PAD // TN),
        in_specs=[
            pl.BlockSpec((tm, D_MODEL), lambda i, j: (row_block0 + i, 0)),
            pl.BlockSpec((1, D_MODEL), lambda i, j: (0, 0)),
            pl.BlockSpec((None, D_MODEL, TN), lambda i, j: (l, 0, j)),
        ],
        out_specs=pl.BlockSpec((tm, TN), lambda i, j: (i, j)),
        out_shape=jax.ShapeDtypeStruct((rows, N_IN_PAD), F32),
        scratch_shapes=[pltpu.VMEM((tm, D_MODEL), BF16)],
        compiler_params=pltpu.CompilerParams(
            dimension_semantics=("parallel", "arbitrary"), vmem_limit_bytes=VMEM_LIMIT),
        name="inproj",
    )(x, nw, w)


TAIL_P = ROWS_P - (ROWS // TM - 1) * TM


def _outproj_kernel(x_ref, yap_ref, ybp_ref, yas_ref, ybs_ref, w_ref, o_ref):
    def project(ya, yb):
        o_ref[...] = x_ref[...] + _dot(ya, w_ref[0:LRU_WIDTH, :]) + _dot(yb, w_ref[LRU_WIDTH:, :])

    last = pl.num_programs(0) - 1

    @pl.when(pl.program_id(0) < last)
    def _():
        project(yap_ref[...], ybp_ref[...])

    @pl.when(pl.program_id(0) == last)
    def _():
        project(jnp.concatenate([yap_ref[0:TAIL_P, :], yas_ref[...]], axis=0),
                jnp.concatenate([ybp_ref[0:TAIL_P, :], ybs_ref[...]], axis=0))


def _outproj(x, ya_p, yb_p, ya_s, yb_s, w, l):
    assert TAIL_P + ROWS_S == TM
    return pl.pallas_call(
        _outproj_kernel,
        grid=(ROWS // TM,),
        in_specs=[
            pl.BlockSpec((TM, D_MODEL), lambda i: (i, 0)),
            pl.BlockSpec((TM, LRU_WIDTH), lambda i: (i, 0)),
            pl.BlockSpec((TM, GDN_VW), lambda i: (i, 0)),
            pl.BlockSpec((ROWS_S, LRU_WIDTH), lambda i: (0, 0)),
            pl.BlockSpec((ROWS_S, GDN_VW), lambda i: (0, 0)),
            pl.BlockSpec((None, D_MODEL, D_MODEL), lambda i: (l, 0, 0)),
        ],
        out_specs=pl.BlockSpec((TM, D_MODEL), lambda i: (i, 0)),
        out_shape=jax.ShapeDtypeStruct((ROWS, D_MODEL), F32),
        compiler_params=pltpu.CompilerParams(
            dimension_semantics=("parallel",), vmem_limit_bytes=VMEM_LIMIT),
        name="outproj",
    )(x, ya_p, yb_p, ya_s, yb_s, w)


def _conv_from_scratch(xe_ref, cw, tt):
    width = xe_ref.shape[-1]
    groups = tt // SUBLANES + 1
    x3 = xe_ref[...].reshape(groups, SUBLANES, width)
    row = lax.broadcasted_iota(jnp.int32, (1, SUBLANES, width), 1)
    y = cw[CONV_W - 1:CONV_W][None] * x3[1:]
    for s in range(1, CONV_W):
        rot = pltpu.roll(x3, s, 1)
        y = y + cw[CONV_W - 1 - s:CONV_W - s][None] * jnp.where(row < s, rot[:-1], rot[1:])
    return y.reshape(tt, width)


def _lru_rows(xa, ga, cw_ref, cb_ref, wg_ref, rgb_ref, igb_ref, lam_ref, na_ref, xe_ref, a_ref, b_ref, hc_ref,
              tt, unrolled):
    xe_ref[SUBLANES:tt + SUBLANES, :] = xa
    xc = _conv_from_scratch(xe_ref, cw_ref[...], tt) + cb_ref[...]

    gw = 4 * LRU_BLOCK
    r_parts, i_parts = [], []
    for q in range(LRU_WIDTH // gw):
        gg = _dot(xc[:, q * gw:(q + 1) * gw].astype(BF16), wg_ref[q])
        r_parts.append(gg[:, :gw])
        i_parts.append(gg[:, gw:])
    r = jax.nn.sigmoid(jnp.concatenate(r_parts, axis=1) + rgb_ref[...])
    ig = jax.nn.sigmoid(jnp.concatenate(i_parts, axis=1) + igb_ref[...])
    log_a = (-LRU_C) * r * _softplus(-lam_ref[...])
    a = jnp.exp(log_a)
    a_ref[...] = a
    b_ref[...] = jnp.sqrt(-jnp.tanh(log_a) * (a * a + 1.0)) * ig * xc

    row = lax.broadcasted_iota(jnp.int32, (SUBLANES, LRU_WIDTH), 0)

    def group(gi, hc):
        off = gi * SUBLANES if unrolled else pl.multiple_of(gi * SUBLANES, SUBLANES)
        a8 = a_ref[pl.ds(off, SUBLANES), :]
        b8 = b_ref[pl.ds(off, SUBLANES), :]
        for k in (1, 2, 4):
            keep = row >= k
            a_prev = jnp.where(keep, pltpu.roll(a8, k, 0), 1.0)
            b_prev = jnp.where(keep, pltpu.roll(b8, k, 0), 0.0)
            b8 = a8 * b_prev + b8
            a8 = a8 * a_prev
        h8 = a8 * hc + b8
        a_ref[pl.ds(off, SUBLANES), :] = h8
        return h8[SUBLANES - 1:SUBLANES, :]

    if unrolled:
        hc = hc_ref[...]
        for gi in range(tt // SUBLANES):
            hc = group(gi, hc)
        hc_ref[...] = hc
    else:
        hc_ref[...] = lax.fori_loop(0, tt // SUBLANES, group, hc_ref[...])

    xe_ref[0:SUBLANES, :] = xe_ref[tt:tt + SUBLANES, :]
    return (_rms(a_ref[...], na_ref[...]) * jax.nn.gelu(ga, approximate=True)).astype(BF16)


def _lru_kernel(xa_ref, ga_ref, halo_ref, h0_ref, cw_ref, cb_ref, wg_ref, rgb_ref, igb_ref, lam_ref, na_ref,
                ya_ref, last_ref, xe_ref, a_ref, b_ref, hc_ref, *, tt, last_tile, last_row):
    t = pl.program_id(1)

    @pl.when(t == 0)
    def _():
        xe_ref[0:SUBLANES, :] = halo_ref[0]
        hc_ref[...] = h0_ref[0]

    ya_ref[...] = _lru_rows(xa_ref[...], ga_ref[...], cw_ref, cb_ref, wg_ref, rgb_ref, igb_ref, lam_ref, na_ref,
                            xe_ref, a_ref, b_ref, hc_ref, tt, False)

    @pl.when(t == last_tile)
    def _():
        last_ref[0] = a_ref[last_row:last_row + 1, :]


def _lru(proj, halo, h0, cw, cb, wg, rgb, igb, lam, na, *, nb, tt, nt, row_block0, last_tile, last_row):
    row_map = lambda col: (lambda b, t: (row_block0 + b * nt + t, col))
    const2 = lambda b, t: (0, 0)
    return pl.pallas_call(
        functools.partial(_lru_kernel, tt=tt, last_tile=last_tile, last_row=last_row),
        grid=(nb, nt),
        in_specs=[
            pl.BlockSpec((tt, LRU_WIDTH), row_map(COL_XA)),
            pl.BlockSpec((tt, LRU_WIDTH), row_map(COL_GA)),
            pl.BlockSpec((1, SUBLANES, LRU_WIDTH), lambda b, t: (b, 0, 0)),
            pl.BlockSpec((1, 1, LRU_WIDTH), lambda b, t: (b, 0, 0)),
            pl.BlockSpec((SUBLANES, LRU_WIDTH), const2),
            pl.BlockSpec((1, LRU_WIDTH), const2),
            pl.BlockSpec((4, 4 * LRU_BLOCK, 8 * LRU_BLOCK), lambda b, t: (0, 0, 0)),
            pl.BlockSpec((1, LRU_WIDTH), const2),
            pl.BlockSpec((1, LRU_WIDTH), const2),
            pl.BlockSpec((1, LRU_WIDTH), const2),
            pl.BlockSpec((1, LRU_WIDTH), const2),
        ],
        out_specs=[
            pl.BlockSpec((tt, LRU_WIDTH), lambda b, t: (b * nt + t, 0)),
            pl.BlockSpec((1, 1, LRU_WIDTH), lambda b, t: (b, 0, 0)),
        ],
        out_shape=[
            jax.ShapeDtypeStruct((nb * nt * tt, LRU_WIDTH), BF16),
            jax.ShapeDtypeStruct((nb, 1, LRU_WIDTH), F32),
        ],
        scratch_shapes=[
            pltpu.VMEM((tt + SUBLANES, LRU_WIDTH), F32),
            pltpu.VMEM((tt, LRU_WIDTH), F32),
            pltpu.VMEM((tt, LRU_WIDTH), F32),
            pltpu.VMEM((1, LRU_WIDTH), F32),
        ],
        compiler_params=pltpu.CompilerParams(
            dimension_semantics=("parallel", "arbitrary"), vmem_limit_bytes=VMEM_LIMIT),
        name="rglru",
    )(proj, proj, halo, h0, cw, cb, wg, rgb, igb, lam, na)


NJ = N_IN_PAD // TN
SUB = TT_LRU // NJ
NT_P = ROWS_P // TT_LRU
TILES_PER_STREAM = TP // TT_LRU
XG_COLS = 2 * LRU_WIDTH


def _front_kernel(x_ref, nw_ref, w_ref, cw_ref, cb_ref, wg_ref, rgb_ref, igb_ref, lam_ref, na_ref,
                  proj_ref, ya_ref, last_ref, h_ref, xg_ref, xe_ref, a_ref, b_ref, hc_ref):
    t = pl.program_id(0)
    j = pl.program_id(1)
    slot = t % 2

    @pl.when(j == 0)
    def _():
        h_ref[...] = _rms(x_ref[...], nw_ref[...]).astype(BF16)

    @pl.when((j == 0) & (t == 0))
    def _():
        xg_ref[...] = jnp.zeros(xg_ref.shape, F32)
        xe_ref[0:SUBLANES, :] = jnp.zeros((SUBLANES, LRU_WIDTH), F32)
        hc_ref[...] = jnp.zeros(hc_ref.shape, F32)

    @pl.when((j == 0) & ((t + TILES_PER_STREAM - 1) % TILES_PER_STREAM == 0))
    def _():
        xe_ref[0:SUBLANES, :] = jnp.zeros((SUBLANES, LRU_WIDTH), F32)
        hc_ref[...] = jnp.zeros(hc_ref.shape, F32)

    proj_ref[...] = _dot(h_ref[...], w_ref[...])
    r0 = pl.multiple_of(j * SUB, SUB)
    ya_ref[...] = _lru_rows(xg_ref[1 - slot, pl.ds(r0, SUB), 0:LRU_WIDTH],
                            xg_ref[1 - slot, pl.ds(r0, SUB), LRU_WIDTH:XG_COLS],
                            cw_ref, cb_ref, wg_ref, rgb_ref, igb_ref, lam_ref, na_ref,
                            xe_ref, a_ref, b_ref, hc_ref, SUB, True)

    @pl.when(j == 0)
    def _():
        xg_ref[slot, :, 0:TN] = proj_ref[...]

    @pl.when(j == 1)
    def _():
        xg_ref[slot, :, TN:XG_COLS] = proj_ref[:, 0:XG_COLS - TN]

    last_tile, last_row = (VALID_P - 1) // TT_LRU, (VALID_P - 1) % TT_LRU
    for b in range(BATCH):
        @pl.when((t - 1 == b * TILES_PER_STREAM + last_tile) & (j == last_row // SUB))
        def _():
            last_ref[b] = a_ref[last_row % SUB:last_row % SUB + 1, :]


def _front(x, nw, w, l, cw, cb, wg, rgb, igb, lam, na):
    assert TN < XG_COLS <= 2 * TN and TT_LRU % NJ == 0 and SUB % SUBLANES == 0
    const2 = lambda t, j: (0, 0)
    return pl.pallas_call(
        _front_kernel,
        grid=(NT_P + 1, NJ),
        in_specs=[
            pl.BlockSpec((TT_LRU, D_MODEL), lambda t, j: (jnp.minimum(t, NT_P - 1), 0)),
            pl.BlockSpec((1, D_MODEL), const2),
            pl.BlockSpec((None, D_MODEL, TN), lambda t, j: (l, 0, j)),
            pl.BlockSpec((SUBLANES, LRU_WIDTH), const2),
            pl.BlockSpec((1, LRU_WIDTH), const2),
            pl.BlockSpec((4, 4 * LRU_BLOCK, 8 * LRU_BLOCK), lambda t, j: (0, 0, 0)),
            pl.BlockSpec((1, LRU_WIDTH), const2),
            pl.BlockSpec((1, LRU_WIDTH), const2),
            pl.BlockSpec((1, LRU_WIDTH), const2),
            pl.BlockSpec((1, LRU_WIDTH), const2),
        ],
        out_specs=[
            pl.BlockSpec((TT_LRU, TN), lambda t, j: (jnp.minimum(t, NT_P - 1), j)),
            pl.BlockSpec((SUB, LRU_WIDTH), lambda t, j: (jnp.maximum((t - 1) * NJ + j, 0), 0)),
            pl.BlockSpec((BATCH, 1, LRU_WIDTH), lambda t, j: (0, 0, 0)),
        ],
        out_shape=[
            jax.ShapeDtypeStruct((ROWS_P, N_IN_PAD), F32),
            jax.ShapeDtypeStruct((ROWS_P, LRU_WIDTH), BF16),
            jax.ShapeDtypeStruct((BATCH, 1, LRU_WIDTH), F32),
        ],
        scratch_shapes=[
            pltpu.VMEM((TT_LRU, D_MODEL), BF16),
            pltpu.VMEM((2, TT_LRU, XG_COLS), F32),
            pltpu.VMEM((SUB + SUBLANES, LRU_WIDTH), F32),
            pltpu.VMEM((SUB, LRU_WIDTH), F32),
            pltpu.VMEM((SUB, LRU_WIDTH), F32),
            pltpu.VMEM((1, LRU_WIDTH), F32),
        ],
        compiler_params=pltpu.CompilerParams(
            dimension_semantics=("arbitrary", "arbitrary"), vmem_limit_bytes=VMEM_LIMIT),
        name="front",
    )(x, nw, w, cw, cb, wg, rgb, igb, lam, na)


def _split_bf16(x):
    hi = x.astype(BF16)
    lo = (x - hi.astype(F32)).astype(BF16)
    return hi, lo


def _pdot(x, y):
    (xh, xl), (yh, yl) = x, y
    lhs = jnp.concatenate([xh, xl, xh], axis=1)
    rhs = jnp.concatenate([yh, yh, yl], axis=0)
    return _dot(lhs, rhs)


N_LEVELS = 4
HEAD_GROUP = 8


def _inverse_masks():
    ri = lax.broadcasted_iota(jnp.int32, (CHUNK, CHUNK), 0)
    ci = lax.broadcasted_iota(jnp.int32, (CHUNK, CHUNK), 1)
    masks = [ri == ci, (ri // SUBLANES) == (ci // SUBLANES)]
    m = SUBLANES
    while m < CHUNK:
        masks.append(((ri // (2 * m)) == (ci // (2 * m))) & ((ri // m) != (ci // m)) & (ri > ci))
        m *= 2
    return jnp.stack(masks, axis=0).astype(BF16)


def _unit_lower_inverses(a_list, mask_ref):
    eye = mask_ref[0]
    blk = mask_ref[1]
    a_s = [_split_bf16(a) for a in a_list]
    a0_s = [(ah * blk, al * blk) for ah, al in a_s]
    p2_s = [_split_bf16(_pdot(a0, a0)) for a0 in a0_s]
    p4_s = [_split_bf16(_pdot(p2, p2)) for p2 in p2_s]
    t1_s = [_split_bf16(_pdot((eye - a0h, -a0l), (eye + p2h, p2l))) for (a0h, a0l), (p2h, p2l) in zip(a0_s, p2_s)]
    t_list = [_pdot(t1, (eye + p4h, p4l)) for t1, (p4h, p4l) in zip(t1_s, p4_s)]
    for lvl in range(N_LEVELS):
        sub = mask_ref[2 + lvl]
        t_s = [_split_bf16(t) for t in t_list]
        et_s = [_split_bf16(_pdot((ah * sub, al * sub), ts)) for (ah, al), ts in zip(a_s, t_s)]
        t_list = [t - _pdot(ts, et) for t, ts, et in zip(t_list, t_s, et_s)]
    return t_list


def _pad_rows(x, rows):
    if x.shape[0] == rows:
        return x
    return jnp.concatenate([x, jnp.zeros((rows - x.shape[0], x.shape[1]), x.dtype)], axis=0)


def _gdn_recurrence(lhs1_ref, lhs2_ref, u_ref, ge_ref, z_ref, nb_ref, yb_ref, s_ref, tr):
    for h in range(GDN_HEADS):
        s = s_ref[h]
        m1 = _dot(lhs1_ref[h], s.astype(BF16))
        w = u_ref[h] - m1[:CHUNK]
        m2 = _dot(lhs2_ref[h], w.astype(BF16))
        o = (m1[CHUNK:] + m2[:CHUNK])[:tr]
        s_ref[h] = ge_ref[h:h + 1, :] * s + m2[CHUNK:]
        sl = slice(h * GDN_DV, (h + 1) * GDN_DV)
        yb_ref[:, sl] = (_rms(o, nb_ref[...]) * _silu(z_ref[:, sl])).astype(BF16)


def _gdn_kernel(q_ref, k_ref, v_ref, tail_ref, z_ref, halo_ref, s0_ref, cw_ref, arow_ref, dtrow_ref, mask_ref,
                nb_ref, yb_ref, sout_ref, xe_ref, s_ref, lhs1_ref, lhs2_ref, u_ref, ge_ref,
                *, tr, nc, valid, overlap):
    c = pl.program_id(1)
    srcs = (q_ref, k_ref, v_ref)

    @pl.when(c == 0)
    def _():
        for i in range(3):
            xe_ref[i, 0:SUBLANES, :] = halo_ref[0, :, i * GDN_QK:(i + 1) * GDN_QK]
        s_ref[...] = s0_ref[0]
        lhs1_ref[...] = jnp.zeros(lhs1_ref.shape, BF16)
        lhs2_ref[...] = jnp.zeros(lhs2_ref.shape, BF16)
        u_ref[...] = jnp.zeros(u_ref.shape, F32)
        ge_ref[...] = jnp.ones(ge_ref.shape, F32)

    @pl.when(c > 0)
    def _():
        for i in range(3):
            xe_ref[i, 0:SUBLANES, :] = xe_ref[i, tr:tr + SUBLANES, :]

    recurrence = functools.partial(_gdn_recurrence, lhs1_ref, lhs2_ref, u_ref, ge_ref, z_ref, nb_ref, yb_ref,
                                   s_ref, tr)
    prepare = functools.partial(_gdn_prepare, srcs, tail_ref, cw_ref, arow_ref, dtrow_ref, mask_ref,
                                lhs1_ref, lhs2_ref, u_ref, ge_ref, xe_ref, tr, valid - jnp.minimum(c, nc - 1) * tr)
    if overlap:
        recurrence()
        prepare()
    else:
        pl.when(c > 0)(recurrence)
        pl.when(c < nc)(prepare)

    @pl.when(c == nc)
    def _():
        sout_ref[0] = s_ref[...]


def _gdn_prepare(srcs, tail_ref, cw_ref, arow_ref, dtrow_ref, mask_ref, lhs1_ref, lhs2_ref, u_ref, ge_ref, xe_ref,
                 tr, rows_left):
    qkv = []
    for i in range(3):
        xe_ref[i, SUBLANES:tr + SUBLANES, :] = srcs[i][...]
        y = _conv_from_scratch(xe_ref.at[i], cw_ref[:, i * GDN_QK:(i + 1) * GDN_QK], tr)
        qkv.append(_pad_rows(_silu(y), CHUNK))
    q_all, k_all, v_all = qkv

    tail = tail_ref[...]
    live =(lax.broadcasted_iota(jnp.int32, (tr, LANES), 0) < rows_left).astype(F32)
    beta = _pad_rows(jax.nn.sigmoid(tail) * live, CHUNK)
    g = _pad_rows(-jnp.exp(arow_ref[...]) * _softplus(tail + dtrow_ref[...]) * live, CHUNK)

    ri = lax.broadcasted_iota(jnp.int32, (CHUNK, CHUNK), 0)
    ci = lax.broadcasted_iota(jnp.int32, (CHUNK, CHUNK), 1)
    incl = ri >= ci
    strict = ri > ci
    gc = jnp.dot(incl.astype(F32), g, precision=lax.Precision.HIGHEST, preferred_element_type=F32)
    gc_t = gc.T
    ge_ref[...] = jnp.broadcast_to(
        jnp.exp(gc_t[LANE_G:LANE_G + GDN_HEADS, CHUNK - 1:CHUNK]), (GDN_HEADS, LANES))

    for h0 in range(0, GDN_HEADS, HEAD_GROUP):
        _gdn_prepare_heads(range(h0, h0 + HEAD_GROUP), q_all, k_all, v_all, gc, gc_t, beta, incl, strict,
                           mask_ref, lhs1_ref, lhs2_ref, u_ref)


def _gdn_prepare_heads(heads, q_all, k_all, v_all, gc, gc_t, beta, incl, strict, mask_ref, lhs1_ref, lhs2_ref, u_ref):
    a_list, rhs_list = [], []
    for h in heads:
        sl = slice(h * GDN_DK, (h + 1) * GDN_DK)
        qh, kh, vh = q_all[:, sl], k_all[:, sl], v_all[:, sl]
        qh = qh * lax.rsqrt(jnp.sum(qh * qh, axis=-1, keepdims=True) + EPS) * (GDN_DK ** -0.5)
        kh = kh * lax.rsqrt(jnp.sum(kh * kh, axis=-1, keepdims=True) + EPS)
        gcol = gc[:, LANE_G + h:LANE_G + h + 1]
        grow = gc_t[LANE_G + h:LANE_G + h + 1, :]
        bcol = beta[:, LANE_BETA + h:LANE_BETA + h + 1]
        decay = jnp.where(incl, jnp.exp(jnp.where(incl, gcol - grow, 0.0)), 0.0)
        kb = kh.astype(BF16)
        qkk = lax.dot_general(jnp.concatenate([qh.astype(BF16), kb], axis=0), kb,
                              (((1,), (1,)), ((), ())), preferred_element_type=F32)
        a_list.append(jnp.where(strict, bcol * decay * qkk[CHUNK:], 0.0))
        egc = jnp.exp(gcol)
        rhs_list.append(jnp.concatenate([bcol * vh, (bcol * egc) * kh], axis=1))
        lhs1_ref[h, CHUNK:, :] = (qh * egc).astype(BF16)
        k_end = kh * jnp.exp(grow[:, CHUNK - 1:CHUNK] - gcol)
        lhs2_ref[h, 0:CHUNK, :] = (qkk[:CHUNK] * decay).astype(BF16)
        lhs2_ref[h, CHUNK:, :] = k_end.T.astype(BF16)

    t_list = _unit_lower_inverses(a_list, mask_ref)
    for i, h in enumerate(heads):
        sol = _pdot(_split_bf16(t_list[i]), _split_bf16(rhs_list[i]))
        u_ref[h] = sol[:, :GDN_DV]
        lhs1_ref[h, 0:CHUNK, :] = sol[:, GDN_DV:].astype(BF16)


def _gdn(proj, halo, s0, cw, arow, dtrow, nbw, *, nb, tr, nc, row_block0, valid, overlap):
    prep_map = lambda col: (lambda b, c: (row_block0 + b * nc + jnp.minimum(c, nc - 1), col))
    const2 = lambda b, c: (0, 0)
    return pl.pallas_call(
        functools.partial(_gdn_kernel, tr=tr, nc=nc, valid=valid, overlap=overlap),
        grid=(nb, nc + 1),
        in_specs=[
            pl.BlockSpec((tr, GDN_QK), prep_map(COL_Q)),
            pl.BlockSpec((tr, GDN_QK), prep_map(COL_K)),
            pl.BlockSpec((tr, GDN_VW), prep_map(COL_V)),
            pl.BlockSpec((tr, LANES), prep_map(COL_TAIL)),
            pl.BlockSpec((tr, GDN_VW), lambda b, c: (row_block0 + b * nc + jnp.maximum(c - 1, 0), COL_Z)),
            pl.BlockSpec((1, SUBLANES, GDN_QKV), lambda b, c: (b, 0, 0)),
            pl.BlockSpec((1, GDN_HEADS, GDN_DK, GDN_DV), lambda b, c: (b, 0, 0, 0)),
            pl.BlockSpec((SUBLANES, GDN_QKV), const2),
            pl.BlockSpec((1, LANES), const2),
            pl.BlockSpec((1, LANES), const2),
            pl.BlockSpec((2 + N_LEVELS, CHUNK, CHUNK), lambda b, c: (0, 0, 0)),
            pl.BlockSpec((1, GDN_DV), const2),
        ],
        out_specs=[
            pl.BlockSpec((tr, GDN_VW), lambda b, c: (b * nc + jnp.maximum(c - 1, 0), 0)),
            pl.BlockSpec((1, GDN_HEADS, GDN_DK, GDN_DV), lambda b, c: (b, 0, 0, 0)),
        ],
        out_shape=[
            jax.ShapeDtypeStruct((nb * nc * tr, GDN_VW), BF16),
            jax.ShapeDtypeStruct((nb, GDN_HEADS, GDN_DK, GDN_DV), F32),
        ],
        scratch_shapes=[
            pltpu.VMEM((3, tr + SUBLANES, GDN_QK), F32),
            pltpu.VMEM((GDN_HEADS, GDN_DK, GDN_DV), F32),
            pltpu.VMEM((GDN_HEADS, 2 * CHUNK, GDN_DK), BF16),
            pltpu.VMEM((GDN_HEADS, 2 * CHUNK, GDN_DK), BF16),
            pltpu.VMEM((GDN_HEADS, CHUNK, GDN_DV), F32),
            pltpu.VMEM((GDN_HEADS, LANES), F32),
        ],
        compiler_params=pltpu.CompilerParams(
            dimension_semantics=("parallel", "arbitrary"), vmem_limit_bytes=VMEM_LIMIT),
        name="gdn",
    )(proj, proj, proj, proj, proj, halo, s0, cw, arow, dtrow, _inverse_masks(), nbw)


def _halo(hist):
    return jnp.pad(hist, ((0, 0), (SUBLANES - (CONV_W - 1), 0), (0, 0)))


def _taps(w):
    return jnp.pad(w, ((0, SUBLANES - CONV_W), (0, 0)))


def _gate_weights(rg_w, ig_w):
    eye = jnp.eye(4, dtype=F32)

    def bd(w):
        return jnp.einsum('qnij,nm->qnimj', w.reshape(4, 4, LRU_BLOCK, LRU_BLOCK), eye).reshape(
            4, 4 * LRU_BLOCK, 4 * LRU_BLOCK)

    return jnp.concatenate([bd(rg_w), bd(ig_w)], axis=2).astype(BF16)


def _lane_row(vals, lane0):
    return jnp.zeros((1, LANES), F32).at[0, lane0:lane0 + vals.shape[0]].set(vals)


def _mixer(x, l, st, p):
    state_conv_a, state_lru, state_conv_b, state_delta = st
    mix_nw = p['mix_norm'][l][None]
    cw_a, cb_a = _taps(p['conv_a_w'][l]), p['conv_a_b'][l][None]
    wg = _gate_weights(p['rg_w'][l], p['ig_w'][l])
    lru_args = (cw_a, cb_a, wg, p['rg_b'][l][None], p['ig_b'][l][None], p['lru_lambda'][l][None],
                p['norm_a'][l][None])
    cw_b = _taps(p['conv_b_w'][l])
    arow = _lane_row(p['a_log'][l], LANE_G)
    dtrow = _lane_row(p['dt_bias'][l], LANE_G)
    nbw = p['norm_b'][l][None]

    proj_p, ya_p, lru_p = _front(x, mix_nw, p['w_in'], l, *lru_args)
    ncp = TP // CHUNK
    yb_p, dl_p = _gdn(proj_p, jnp.zeros((BATCH, SUBLANES, GDN_QKV), F32),
                      jnp.zeros((BATCH, GDN_HEADS, GDN_DK, GDN_DV), F32), cw_b, arow, dtrow, nbw,
                      nb=BATCH, tr=CHUNK, nc=ncp, row_block0=0, valid=VALID_P, overlap=True)

    proj_s = _inproj(x, mix_nw, p['w_in'], l, tm=ROWS_S, row_block0=ROWS_P // ROWS_S, rows=ROWS_S)
    ya_s, lru_s = _lru(proj_s, _halo(state_conv_a[l]), state_lru[l][:, None, :], *lru_args,
                       nb=DEC_BATCH, tt=DEC_SEQ, nt=1, row_block0=0, last_tile=0, last_row=DEC_SEQ - 1)
    yb_s, dl_s = _gdn(proj_s, _halo(state_conv_b[l]), state_delta[l], cw_b, arow, dtrow, nbw,
                      nb=DEC_BATCH, tr=DEC_SEQ, nc=1, row_block0=0, valid=DEC_SEQ, overlap=False)

    x = _outproj(x, ya_p, yb_p, ya_s, yb_s, p['w_out'], l)

    pp = jnp.stack([proj_p[b * TP + VALID_P - 3:b * TP + VALID_P] for b in range(BATCH)], axis=0)
    ps = proj_s.reshape(DEC_BATCH, DEC_SEQ, N_IN_PAD)[:, DEC_SEQ - 3:]
    o2 = 2 * LRU_WIDTH
    new_p = (pp[..., :LRU_WIDTH], lru_p[:, 0], pp[..., o2:o2 + GDN_QKV], dl_p)
    new_s = (ps[..., :LRU_WIDTH], lru_s[:, 0], ps[..., o2:o2 + GDN_QKV], dl_s)
    return x, new_p, new_s


def kernel(x_prompt, x_sample, state_conv_a, state_lru, state_conv_b, state_delta, meta_tokens, ffn1_norm, ffn1_w_gate, ffn1_w_up, ffn1_w_down, mix_norm, w_in, conv_a_w, conv_a_b, rg_w, rg_b, ig_w, ig_b, lru_lambda, norm_a, conv_b_w, a_log, dt_bias, norm_b, w_out, ffn2_norm, ffn2_w_gate, ffn2_w_up, ffn2_w_down, final_norm):
    w_in_b = jnp.pad(w_in.astype(BF16), ((0, 0), (0, 0), (0, N_IN_PAD - N_IN)))
    p = dict(mix_norm=mix_norm, w_in=w_in_b, conv_a_w=conv_a_w, conv_a_b=conv_a_b, rg_w=rg_w, rg_b=rg_b,
             ig_w=ig_w, ig_b=ig_b, lru_lambda=lru_lambda, norm_a=norm_a, conv_b_w=conv_b_w, a_log=a_log,
             dt_bias=dt_bias, norm_b=norm_b, w_out=w_out.astype(BF16))
    ffn1 = (ffn1_w_gate.astype(BF16), ffn1_w_up.astype(BF16), ffn1_w_down.astype(BF16))
    ffn2 = (ffn2_w_gate.astype(BF16), ffn2_w_up.astype(BF16), ffn2_w_down.astype(BF16))
    pad = jnp.zeros((TP - VALID_P, D_MODEL), F32)
    parts = []
    for b in range(BATCH):
        parts += [meta_tokens, x_prompt[b], pad]
    x = jnp.concatenate(parts + [x_sample.reshape(ROWS_S, D_MODEL)], axis=0)
    st = (state_conv_a, state_lru, state_conv_b, state_delta)
    fw = final_norm[None]
    news_p, news_s = [], []
    for l in range(DEPTH):
        x = _ffn(x, ffn1_norm[l][None], *ffn1, fw, l, False)
        x, new_p, new_s = _mixer(x, l, st, p)
        x = _ffn(x, ffn2_norm[l][None], *ffn2, fw, l, l == DEPTH - 1)
        news_p.append(new_p)
        news_s.append(new_s)
    y_p = jnp.stack([x[b * TP + N_META:b * TP + VALID_P] for b in range(BATCH)], axis=0)
    y_s = x[ROWS_P:].reshape(DEC_BATCH, DEC_SEQ, D_MODEL)
    stack = lambda news, i: jnp.stack([n[i] for n in news], axis=0)
    return (y_p, y_s,
            stack(news_p, 0), stack(news_p, 1), stack(news_p, 2), stack(news_p, 3),
            stack(news_s, 0), stack(news_s, 1), stack(news_s, 2), stack(news_s, 3))
```

```python
import functools

import jax
import jax.numpy as jnp
from jax import lax
from jax.experimental import pallas as pl
from jax.experimental.pallas import tpu as pltpu

F32 = jnp.float32
BF16 = jnp.bfloat16

D_MODEL = 2048
BATCH = 2
SEQ = 8192
DEPTH = 2
DEC_BATCH = 16
DEC_SEQ = 16
N_META = 16
LRU_WIDTH = 1024
LRU_BLOCKS = 16
LRU_BLOCK = 64
LRU_C = 8.0
CONV_W = 4
GDN_HEADS = 8
GDN_DK = 128
GDN_DV = 128
GDN_QK = 1024
GDN_VW = 1024
GDN_QKV = 3072
N_IN = 6160
D_FF = 5632
EPS = 1e-6

LANES = 128
SUBLANES = 8
CHUNK = 128
VALID_P = N_META + SEQ
TP = 8320
ROWS_P = BATCH * TP
ROWS_S = DEC_BATCH * DEC_SEQ
ROWS = ROWS_P + ROWS_S
N_IN_PAD = 6400
COL_XA, COL_GA, COL_Q, COL_K, COL_V, COL_Z = 0, 1, 2, 3, 4, 5
COL_TAIL = 48
LANE_BETA = 0
LANE_G = 8

TM = 768
TF = 512
TN = 1280
TT_LRU = 640
VMEM_LIMIT = 56 * 1024 * 1024


def _rms(x, w):
    return x * lax.rsqrt(jnp.mean(x * x, axis=-1, keepdims=True) + EPS) * w


def _silu(x):
    return x * jax.nn.sigmoid(x)


def _softplus(x):
    return jnp.maximum(x, 0.0) + jnp.log1p(jnp.exp(-jnp.abs(x)))


def _dot(a, b):
    return jnp.dot(a, b, preferred_element_type=F32)


COPY_ROWS = 2048


def _assemble_copies(meta_ref, xp_ref, xs_ref, o_ref, zero_ref, sem):
    copies = []

    def add(src, row0, rows):
        copies.append(pltpu.make_async_copy(src, o_ref.at[pl.ds(row0, rows)], sem.at[len(copies)]))

    for b in range(BATCH):
        base = b * TP
        add(meta_ref, base, N_META)
        for r in range(0, SEQ, COPY_ROWS):
            add(xp_ref.at[b, pl.ds(r, COPY_ROWS)], base + N_META + r, COPY_ROWS)
        add(zero_ref, base + VALID_P, TP - VALID_P)
    add(xs_ref, ROWS_P, ROWS_S)
    return copies


N_ASSEMBLE_COPIES = BATCH * (2 + SEQ // COPY_ROWS) + 1


def _assemble_kernel(meta_ref, xp_ref, xs_ref, o_ref, zero_ref, sem):
    zero_ref[...] = jnp.zeros(zero_ref.shape, F32)
    copies = _assemble_copies(meta_ref, xp_ref, xs_ref, o_ref, zero_ref, sem)
    for c in copies:
        c.start()
    for c in copies:
        c.wait()


def _assemble(meta, x_prompt, x_sample):
    any_spec = pl.BlockSpec(memory_space=pl.ANY)
    return pl.pallas_call(
        _assemble_kernel,
        in_specs=[any_spec, any_spec, any_spec],
        out_specs=any_spec,
        out_shape=jax.ShapeDtypeStruct((ROWS, D_MODEL), F32),
        scratch_shapes=[pltpu.VMEM((TP - VALID_P, D_MODEL), F32),
                        pltpu.SemaphoreType.DMA((N_ASSEMBLE_COPIES,))],
        name="assemble",
    )(meta, x_prompt, x_sample)


def _ffn_kernel(x_ref, nw_ref, wg_ref, wu_ref, wd_ref, fw_ref, o_ref, h_ref, *, final_norm):
    j = pl.program_id(1)

    @pl.when(j == 0)
    def _():
        x = x_ref[...]
        h_ref[...] = _rms(x, nw_ref[...]).astype(BF16)
        o_ref[...] = x

    h = h_ref[...]
    g = _dot(h, wg_ref[...])
    u = _dot(h, wu_ref[...])
    a = (_silu(g) * u * 0.5).astype(BF16)
    o_ref[...] += _dot(a, wd_ref[...])

    if final_norm:
        @pl.when(j == pl.num_programs(1) - 1)
        def _():
            o_ref[...] = _rms(o_ref[...], fw_ref[...])


def _ffn(x, nw, wg, wu, wd, fw, l, final_norm):
    return pl.pallas_call(
        functools.partial(_ffn_kernel, final_norm=final_norm),
        grid=(ROWS // TM, D_FF // TF),
        in_specs=[
            pl.BlockSpec((TM, D_MODEL), lambda i, j: (i, 0)),
            pl.BlockSpec((1, D_MODEL), lambda i, j: (0, 0)),
            pl.BlockSpec((None, D_MODEL, TF), lambda i, j: (l, 0, j)),
            pl.BlockSpec((None, D_MODEL, TF), lambda i, j: (l, 0, j)),
            pl.BlockSpec((None, TF, D_MODEL), lambda i, j: (l, j, 0)),
            pl.BlockSpec((1, D_MODEL), lambda i, j: (0, 0)),
        ],
        out_specs=pl.BlockSpec((TM, D_MODEL), lambda i, j: (i, 0)),
        out_shape=jax.ShapeDtypeStruct((ROWS, D_MODEL), F32),
        scratch_shapes=[pltpu.VMEM((TM, D_MODEL), BF16)],
        compiler_params=pltpu.CompilerParams(
            dimension_semantics=("parallel", "arbitrary"), vmem_limit_bytes=VMEM_LIMIT),
        name="ffn",
    )(x, nw, wg, wu, wd, fw)


def _inproj_kernel(x_ref, nw_ref, w_ref, o_ref, h_ref):
    @pl.when(pl.program_id(1) == 0)
    def _():
        h_ref[...] = _rms(x_ref[...], nw_ref[...]).astype(BF16)

    o_ref[...] = _dot(h_ref[...], w_ref[...])


def _inproj(x, nw, w, l, *, tm, row_block0, rows):
    return pl.pallas_call(
        _inproj_kernel,
        grid=(rows // tm, N_IN_PAD // TN),
        in_specs=[
            pl.BlockSpec((tm, D_MODEL), lambda i, j: (row_block0 + i, 0)),
            pl.BlockSpec((1, D_MODEL), lambda i, j: (0, 0)),
            pl.BlockSpec((None, D_MODEL, TN), lambda i, j: (l, 0, j)),
        ],
        out_specs=pl.BlockSpec((tm, TN), lambda i, j: (i, j)),
        out_shape=jax.ShapeDtypeStruct((rows, N_IN_PAD), F32),
        scratch_shapes=[pltpu.VMEM((tm, D_MODEL), BF16)],
        compiler_params=pltpu.CompilerParams(
            dimension_semantics=("parallel", "arbitrary"), vmem_limit_bytes=VMEM_LIMIT),
        name="inproj",
    )(x, nw, w)


TAIL_P = ROWS_P - (ROWS // TM - 1) * TM


def _outproj_kernel(x_ref, yap_ref, ybp_ref, yas_ref, ybs_ref, w_ref, o_ref):
    def project(ya, yb):
        o_ref[...] = x_ref[...] + _dot(ya, w_ref[0:LRU_WIDTH, :]) + _dot(yb, w_ref[LRU_WIDTH:, :])

    last = pl.num_programs(0) - 1

    @pl.when(pl.program_id(0) < last)
    def _():
        project(yap_ref[...], ybp_ref[...])

    @pl.when(pl.program_id(0) == last)
    def _():
        project(jnp.concatenate([yap_ref[0:TAIL_P, :], yas_ref[...]], axis=0),
                jnp.concatenate([ybp_ref[0:TAIL_P, :], ybs_ref[...]], axis=0))


def _outproj(x, ya_p, yb_p, ya_s, yb_s, w, l):
    assert TAIL_P + ROWS_S == TM
    return pl.pallas_call(
        _outproj_kernel,
        grid=(ROWS // TM,),
        in_specs=[
            pl.BlockSpec((TM, D_MODEL), lambda i: (i, 0)),
            pl.BlockSpec((TM, LRU_WIDTH), lambda i: (i, 0)),
            pl.BlockSpec((TM, GDN_VW), lambda i: (i, 0)),
            pl.BlockSpec((ROWS_S, LRU_WIDTH), lambda i: (0, 0)),
            pl.BlockSpec((ROWS_S, GDN_VW), lambda i: (0, 0)),
            pl.BlockSpec((None, D_MODEL, D_MODEL), lambda i: (l, 0, 0)),
        ],
        out_specs=pl.BlockSpec((TM, D_MODEL), lambda i: (i, 0)),
        out_shape=jax.ShapeDtypeStruct((ROWS, D_MODEL), F32),
        compiler_params=pltpu.CompilerParams(
            dimension_semantics=("parallel",), vmem_limit_bytes=VMEM_LIMIT),
        name="outproj",
    )(x, ya_p, yb_p, ya_s, yb_s, w)


def _conv_from_scratch(xe_ref, cw, tt):
    width = xe_ref.shape[-1]
    groups = tt // SUBLANES + 1
    x3 = xe_ref[...].reshape(groups, SUBLANES, width)
    row = lax.broadcasted_iota(jnp.int32, (1, SUBLANES, width), 1)
    y = cw[CONV_W - 1:CONV_W][None] * x3[1:]
    for s in range(1, CONV_W):
        rot = pltpu.roll(x3, s, 1)
        y = y + cw[CONV_W - 1 - s:CONV_W - s][None] * jnp.where(row < s, rot[:-1], rot[1:])
    return y.reshape(tt, width)


def _lru_rows(xa, ga, cw_ref, cb_ref, wg_ref, rgb_ref, igb_ref, lam_ref, na_ref, xe_ref, a_ref, b_ref, hc_ref,
              tt, unrolled):
    xe_ref[SUBLANES:tt + SUBLANES, :] = xa
    xc = _conv_from_scratch(xe_ref, cw_ref[...], tt) + cb_ref[...]

    gw = 4 * LRU_BLOCK
    r_parts, i_parts = [], []
    for q in range(LRU_WIDTH // gw):
        gg = _dot(xc[:, q * gw:(q + 1) * gw].astype(BF16), wg_ref[q])
        r_parts.append(gg[:, :gw])
        i_parts.append(gg[:, gw:])
    r = jax.nn.sigmoid(jnp.concatenate(r_parts, axis=1) + rgb_ref[...])
    ig = jax.nn.sigmoid(jnp.concatenate(i_parts, axis=1) + igb_ref[...])
    log_a = (-LRU_C) * r * _softplus(-lam_ref[...])
    a = jnp.exp(log_a)
    a_ref[...] = a
    b_ref[...] = jnp.sqrt(-jnp.tanh(log_a) * (a * a + 1.0)) * ig * xc

    row = lax.broadcasted_iota(jnp.int32, (SUBLANES, LRU_WIDTH), 0)

    def group(gi, hc):
        off = gi * SUBLANES if unrolled else pl.multiple_of(gi * SUBLANES, SUBLANES)
        a8 = a_ref[pl.ds(off, SUBLANES), :]
        b8 = b_ref[pl.ds(off, SUBLANES), :]
        for k in (1, 2, 4):
            keep = row >= k
            a_prev = jnp.where(keep, pltpu.roll(a8, k, 0), 1.0)
            b_prev = jnp.where(keep, pltpu.roll(b8, k, 0), 0.0)
            b8 = a8 * b_prev + b8
            a8 = a8 * a_prev
        h8 = a8 * hc + b8
        a_ref[pl.ds(off, SUBLANES), :] = h8
        return h8[SUBLANES - 1:SUBLANES, :]

    if unrolled:
        hc = hc_ref[...]
        for gi in range(tt // SUBLANES):
            hc = group(gi, hc)
        hc_ref[...] = hc
    else:
        hc_ref[...] = lax.fori_loop(0, tt // SUBLANES, group, hc_ref[...])

    xe_ref[0:SUBLANES, :] = xe_ref[tt:tt + SUBLANES, :]
    return (_rms(a_ref[...], na_ref[...]) * jax.nn.gelu(ga, approximate=True)).astype(BF16)


def _lru_kernel(xa_ref, ga_ref, halo_ref, h0_ref, cw_ref, cb_ref, wg_ref, rgb_ref, igb_ref, lam_ref, na_ref,
                ya_ref, last_ref, xe_ref, a_ref, b_ref, hc_ref, *, tt, last_tile, last_row):
    t = pl.program_id(1)

    @pl.when(t == 0)
    def _():
        xe_ref[0:SUBLANES, :] = halo_ref[0]
        hc_ref[...] = h0_ref[0]

    ya_ref[...] = _lru_rows(xa_ref[...], ga_ref[...], cw_ref, cb_ref, wg_ref, rgb_ref, igb_ref, lam_ref, na_ref,
                            xe_ref, a_ref, b_ref, hc_ref, tt, False)

    @pl.when(t == last_tile)
    def _():
        last_ref[0] = a_ref[last_row:last_row + 1, :]


def _lru(proj, halo, h0, cw, cb, wg, rgb, igb, lam, na, *, nb, tt, nt, row_block0, last_tile, last_row):
    row_map = lambda col: (lambda b, t: (row_block0 + b * nt + t, col))
    const2 = lambda b, t: (0, 0)
    return pl.pallas_call(
        functools.partial(_lru_kernel, tt=tt, last_tile=last_tile, last_row=last_row),
        grid=(nb, nt),
        in_specs=[
            pl.BlockSpec((tt, LRU_WIDTH), row_map(COL_XA)),
            pl.BlockSpec((tt, LRU_WIDTH), row_map(COL_GA)),
            pl.BlockSpec((1, SUBLANES, LRU_WIDTH), lambda b, t: (b, 0, 0)),
            pl.BlockSpec((1, 1, LRU_WIDTH), lambda b, t: (b, 0, 0)),
            pl.BlockSpec((SUBLANES, LRU_WIDTH), const2),
            pl.BlockSpec((1, LRU_WIDTH), const2),
            pl.BlockSpec((4, 4 * LRU_BLOCK, 8 * LRU_BLOCK), lambda b, t: (0, 0, 0)),
            pl.BlockSpec((1, LRU_WIDTH), const2),
            pl.BlockSpec((1, LRU_WIDTH), const2),
            pl.BlockSpec((1, LRU_WIDTH), const2),
            pl.BlockSpec((1, LRU_WIDTH), const2),
        ],
        out_specs=[
            pl.BlockSpec((tt, LRU_WIDTH), lambda b, t: (b * nt + t, 0)),
            pl.BlockSpec((1, 1, LRU_WIDTH), lambda b, t: (b, 0, 0)),
        ],
        out_shape=[
            jax.ShapeDtypeStruct((nb * nt * tt, LRU_WIDTH), BF16),
            jax.ShapeDtypeStruct((nb, 1, LRU_WIDTH), F32),
        ],
        scratch_shapes=[
            pltpu.VMEM((tt + SUBLANES, LRU_WIDTH), F32),
            pltpu.VMEM((tt, LRU_WIDTH), F32),
            pltpu.VMEM((tt, LRU_WIDTH), F32),
            pltpu.VMEM((1, LRU_WIDTH), F32),
        ],
        compiler_params=pltpu.CompilerParams(
            dimension_semantics=("parallel", "arbitrary"), vmem_limit_bytes=VMEM_LIMIT),
        name="rglru",
    )(proj, proj, halo, h0, cw, cb, wg, rgb, igb, lam, na)


NJ = N_IN_PAD // TN
SUB = TT_LRU // NJ
NT_P = ROWS_P // TT_LRU
TILES_PER_STREAM = TP // TT_LRU
XG_COLS = 2 * LRU_WIDTH


def _front_kernel(x_ref, nw_ref, w_ref, cw_ref, cb_ref, wg_ref, rgb_ref, igb_ref, lam_ref, na_ref,
                  proj_ref, ya_ref, last_ref, h_ref, xg_ref, xe_ref, a_ref, b_ref, hc_ref):
    t = pl.program_id(0)
    j = pl.program_id(1)
    slot = t % 2

    @pl.when(j == 0)
    def _():
        h_ref[...] = _rms(x_ref[...], nw_ref[...]).astype(BF16)

    @pl.when((j == 0) & (t == 0))
    def _():
        xg_ref[...] = jnp.zeros(xg_ref.shape, F32)
        xe_ref[0:SUBLANES, :] = jnp.zeros((SUBLANES, LRU_WIDTH), F32)
        hc_ref[...] = jnp.zeros(hc_ref.shape, F32)

    @pl.when((j == 0) & ((t + TILES_PER_STREAM - 1) % TILES_PER_STREAM == 0))
    def _():
        xe_ref[0:SUBLANES, :] = jnp.zeros((SUBLANES, LRU_WIDTH), F32)
        hc_ref[...] = jnp.zeros(hc_ref.shape, F32)

    r0 = pl.multiple_of(j * SUB, SUB)
    ya_ref[...] = _lru_rows(xg_ref[1 - slot, pl.ds(r0, SUB), 0:LRU_WIDTH],
                            xg_ref[1 - slot, pl.ds(r0, SUB), LRU_WIDTH:XG_COLS],
                            cw_ref, cb_ref, wg_ref, rgb_ref, igb_ref, lam_ref, na_ref,
                            xe_ref, a_ref, b_ref, hc_ref, SUB, True)
    proj_ref[...] = _dot(h_ref[...], w_ref[...])

    @pl.when(j == 0)
    def _():
        xg_ref[slot, :, 0:TN] = proj_ref[...]

    @pl.when(j == 1)
    def _():
        xg_ref[slot, :, TN:XG_COLS] = proj_ref[:, 0:XG_COLS - TN]

    last_tile, last_row = (VALID_P - 1) // TT_LRU, (VALID_P - 1) % TT_LRU
    for b in range(BATCH):
        @pl.when((t - 1 == b * TILES_PER_STREAM + last_tile) & (j == last_row // SUB))
        def _():
            last_ref[b] = a_ref[last_row % SUB:last_row % SUB + 1, :]


def _front(x, nw, w, l, cw, cb, wg, rgb, igb, lam, na):
    assert TN < XG_COLS <= 2 * TN and TT_LRU % NJ == 0 and SUB % SUBLANES == 0
    const2 = lambda t, j: (0, 0)
    return pl.pallas_call(
        _front_kernel,
        grid=(NT_P + 1, NJ),
        in_specs=[
            pl.BlockSpec((TT_LRU, D_MODEL), lambda t, j: (jnp.minimum(t, NT_P - 1), 0)),
            pl.BlockSpec((1, D_MODEL), const2),
            pl.BlockSpec((None, D_MODEL, TN), lambda t, j: (l, 0, j)),
            pl.BlockSpec((SUBLANES, LRU_WIDTH), const2),
            pl.BlockSpec((1, LRU_WIDTH), const2),
            pl.BlockSpec((4, 4 * LRU_BLOCK, 8 * LRU_BLOCK), lambda t, j: (0, 0, 0)),
            pl.BlockSpec((1, LRU_WIDTH), const2),
            pl.BlockSpec((1, LRU_WIDTH), const2),
            pl.BlockSpec((1, LRU_WIDTH), const2),
            pl.BlockSpec((1, LRU_WIDTH), const2),
        ],
        out_specs=[
            pl.BlockSpec((TT_LRU, TN), lambda t, j: (t, j)),
            pl.BlockSpec((SUB, LRU_WIDTH), lambda t, j: (jnp.maximum((t - 1) * NJ + j, 0), 0)),
            pl.BlockSpec((BATCH, 1, LRU_WIDTH), lambda t, j: (0, 0, 0)),
        ],
        out_shape=[
            jax.ShapeDtypeStruct((ROWS_P + TT_LRU, N_IN_PAD), F32),
            jax.ShapeDtypeStruct((ROWS_P, LRU_WIDTH), BF16),
            jax.ShapeDtypeStruct((BATCH, 1, LRU_WIDTH), F32),
        ],
        scratch_shapes=[
            pltpu.VMEM((TT_LRU, D_MODEL), BF16),
            pltpu.VMEM((2, TT_LRU, XG_COLS), F32),
            pltpu.VMEM((SUB + SUBLANES, LRU_WIDTH), F32),
            pltpu.VMEM((SUB, LRU_WIDTH), F32),
            pltpu.VMEM((SUB, LRU_WIDTH), F32),
            pltpu.VMEM((1, LRU_WIDTH), F32),
        ],
        compiler_params=pltpu.CompilerParams(
            dimension_semantics=("arbitrary", "arbitrary"), vmem_limit_bytes=VMEM_LIMIT),
        name="front",
    )(x, nw, w, cw, cb, wg, rgb, igb, lam, na)


def _split_bf16(x):
    hi = x.astype(BF16)
    lo = (x - hi.astype(F32)).astype(BF16)
    return hi, lo


def _pdot(x, y):
    (xh, xl), (yh, yl) = x, y
    lhs = jnp.concatenate([xh, xl, xh], axis=1)
    rhs = jnp.concatenate([yh, yh, yl], axis=0)
    return _dot(lhs, rhs)


N_LEVELS = 4
HEAD_GROUP = 8


def _inverse_masks():
    ri = lax.broadcasted_iota(jnp.int32, (CHUNK, CHUNK), 0)
    ci = lax.broadcasted_iota(jnp.int32, (CHUNK, CHUNK), 1)
    masks = [ri == ci, (ri // SUBLANES) == (ci // SUBLANES)]
    m = SUBLANES
    while m < CHUNK:
        masks.append(((ri // (2 * m)) == (ci // (2 * m))) & ((ri // m) != (ci // m)) & (ri > ci))
        m *= 2
    return jnp.stack(masks, axis=0).astype(BF16)


def _unit_lower_inverses(a_list, mask_ref, live_rows):
    eye = mask_ref[0]
    blk = mask_ref[1]
    a_s = [_split_bf16(a) for a in a_list]
    a0_s = [(ah * blk, al * blk) for ah, al in a_s]
    p2_s = [_split_bf16(_pdot(a0, a0)) for a0 in a0_s]
    p4_s = [_split_bf16(_pdot(p2, p2)) for p2 in p2_s]
    t1_s = [_split_bf16(_pdot((eye - a0h, -a0l), (eye + p2h, p2l))) for (a0h, a0l), (p2h, p2l) in zip(a0_s, p2_s)]
    t_list = [_pdot(t1, (eye + p4h, p4l)) for t1, (p4h, p4l) in zip(t1_s, p4_s)]
    for lvl in range(N_LEVELS):
        if SUBLANES << lvl >= live_rows:
            break
        sub = mask_ref[2 + lvl]
        t_s = [_split_bf16(t) for t in t_list]
        et_s = [_split_bf16(_pdot((ah * sub, al * sub), ts)) for (ah, al), ts in zip(a_s, t_s)]
        t_list = [t - _pdot(ts, et) for t, ts, et in zip(t_list, t_s, et_s)]
    return t_list


def _pad_rows(x, rows):
    if x.shape[0] == rows:
        return x
    return jnp.concatenate([x, jnp.zeros((rows - x.shape[0], x.shape[1]), x.dtype)], axis=0)


def _gdn_recurrence(lhs1_ref, lhs2_ref, u_ref, ge_ref, z_ref, nb_ref, yb_ref, s_ref, tr):
    for h in range(GDN_HEADS):
        s = s_ref[h]
        m1 = _dot(lhs1_ref[h], s.astype(BF16))
        w = u_ref[h] - m1[:CHUNK]
        m2 = _dot(lhs2_ref[h], w.astype(BF16))
        o = (m1[CHUNK:] + m2[:CHUNK])[:tr]
        s_ref[h] = ge_ref[h:h + 1, :] * s + m2[CHUNK:]
        sl = slice(h * GDN_DV, (h + 1) * GDN_DV)
        yb_ref[:, sl] = (_rms(o, nb_ref[...]) * _silu(z_ref[:, sl])).astype(BF16)


def _gdn_kernel(q_ref, k_ref, v_ref, tail_ref, z_ref, halo_ref, s0_ref, cw_ref, arow_ref, dtrow_ref, mask_ref,
                nb_ref, yb_ref, sout_ref, xe_ref, s_ref, lhs1_ref, lhs2_ref, u_ref, ge_ref,
                *, tr, nc, valid, overlap):
    c = pl.program_id(1)
    srcs = (q_ref, k_ref, v_ref)

    @pl.when(c == 0)
    def _():
        for i in range(3):
            xe_ref[i, 0:SUBLANES, :] = halo_ref[0, :, i * GDN_QK:(i + 1) * GDN_QK]
        s_ref[...] = s0_ref[0]
        lhs1_ref[...] = jnp.zeros(lhs1_ref.shape, BF16)
        lhs2_ref[...] = jnp.zeros(lhs2_ref.shape, BF16)
        u_ref[...] = jnp.zeros(u_ref.shape, F32)
        ge_ref[...] = jnp.ones(ge_ref.shape, F32)

    @pl.when(c > 0)
    def _():
        for i in range(3):
            xe_ref[i, 0:SUBLANES, :] = xe_ref[i, tr:tr + SUBLANES, :]

    recurrence = functools.partial(_gdn_recurrence, lhs1_ref, lhs2_ref, u_ref, ge_ref, z_ref, nb_ref, yb_ref,
                                   s_ref, tr)
    prepare = functools.partial(_gdn_prepare, srcs, tail_ref, cw_ref, arow_ref, dtrow_ref, mask_ref,
                                lhs1_ref, lhs2_ref, u_ref, ge_ref, xe_ref, tr, valid - jnp.minimum(c, nc - 1) * tr)
    if overlap:
        recurrence()
        prepare()
    else:
        pl.when(c > 0)(recurrence)
        pl.when(c < nc)(prepare)

    @pl.when(c == nc)
    def _():
        sout_ref[0] = s_ref[...]


def _gdn_prepare(srcs, tail_ref, cw_ref, arow_ref, dtrow_ref, mask_ref, lhs1_ref, lhs2_ref, u_ref, ge_ref, xe_ref,
                 tr, rows_left):
    qkv = []
    for i in range(3):
        xe_ref[i, SUBLANES:tr + SUBLANES, :] = srcs[i][...]
        y = _conv_from_scratch(xe_ref.at[i], cw_ref[:, i * GDN_QK:(i + 1) * GDN_QK], tr)
        qkv.append(_pad_rows(_silu(y), CHUNK))
    q_all, k_all, v_all = qkv

    tail = tail_ref[...]
    live =(lax.broadcasted_iota(jnp.int32, (tr, LANES), 0) < rows_left).astype(F32)
    beta = _pad_rows(jax.nn.sigmoid(tail) * live, CHUNK)
    g = _pad_rows(-jnp.exp(arow_ref[...]) * _softplus(tail + dtrow_ref[...]) * live, CHUNK)

    ri = lax.broadcasted_iota(jnp.int32, (CHUNK, CHUNK), 0)
    ci = lax.broadcasted_iota(jnp.int32, (CHUNK, CHUNK), 1)
    incl = ri >= ci
    strict = ri > ci
    gc = jnp.dot(incl.astype(F32), g, precision=lax.Precision.HIGHEST, preferred_element_type=F32)
    gc_t = gc.T
    ge_ref[...] = jnp.broadcast_to(
        jnp.exp(gc_t[LANE_G:LANE_G + GDN_HEADS, CHUNK - 1:CHUNK]), (GDN_HEADS, LANES))

    for h0 in range(0, GDN_HEADS, HEAD_GROUP):
        _gdn_prepare_heads(range(h0, h0 + HEAD_GROUP), q_all, k_all, v_all, gc, gc_t, beta, incl, strict,
                           mask_ref, lhs1_ref, lhs2_ref, u_ref, tr)


def _gdn_prepare_heads(heads, q_all, k_all, v_all, gc, gc_t, beta, incl, strict, mask_ref, lhs1_ref, lhs2_ref, u_ref,
                       live_rows):
    a_list, rhs_list = [], []
    for h in heads:
        sl = slice(h * GDN_DK, (h + 1) * GDN_DK)
        qh, kh, vh = q_all[:, sl], k_all[:, sl], v_all[:, sl]
        qh = qh * lax.rsqrt(jnp.sum(qh * qh, axis=-1, keepdims=True) + EPS) * (GDN_DK ** -0.5)
        kh = kh * lax.rsqrt(jnp.sum(kh * kh, axis=-1, keepdims=True) + EPS)
        gcol = gc[:, LANE_G + h:LANE_G + h + 1]
        grow = gc_t[LANE_G + h:LANE_G + h + 1, :]
        bcol = beta[:, LANE_BETA + h:LANE_BETA + h + 1]
        decay = jnp.where(incl, jnp.exp(jnp.where(incl, gcol - grow, 0.0)), 0.0)
        kb = kh.astype(BF16)
        qkk = lax.dot_general(jnp.concatenate([qh.astype(BF16), kb], axis=0), kb,
                              (((1,), (1,)), ((), ())), preferred_element_type=F32)
        a_list.append(jnp.where(strict, bcol * decay * qkk[CHUNK:], 0.0))
        egc = jnp.exp(gcol)
        rhs_list.append(jnp.concatenate([bcol * vh, (bcol * egc) * kh], axis=1))
        lhs1_ref[h, CHUNK:, :] = (qh * egc).astype(BF16)
        k_end = kh * jnp.exp(grow[:, CHUNK - 1:CHUNK] - gcol)
        lhs2_ref[h, 0:CHUNK, :] = (qkk[:CHUNK] * decay).astype(BF16)
        lhs2_ref[h, CHUNK:, :] = k_end.T.astype(BF16)

    t_list = _unit_lower_inverses(a_list, mask_ref, live_rows)
    for i, h in enumerate(heads):
        sol = _pdot(_split_bf16(t_list[i]), _split_bf16(rhs_list[i]))
        u_ref[h] = sol[:, :GDN_DV]
        lhs1_ref[h, 0:CHUNK, :] = sol[:, GDN_DV:].astype(BF16)


def _gdn(proj, halo, s0, cw, arow, dtrow, nbw, *, nb, tr, nc, row_block0, valid, overlap):
    prep_map = lambda col: (lambda b, c: (row_block0 + b * nc + jnp.minimum(c, nc - 1), col))
    const2 = lambda b, c: (0, 0)
    return pl.pallas_call(
        functools.partial(_gdn_kernel, tr=tr, nc=nc, valid=valid, overlap=overlap),
        grid=(nb, nc + 1),
        in_specs=[
            pl.BlockSpec((tr, GDN_QK), prep_map(COL_Q)),
            pl.BlockSpec((tr, GDN_QK), prep_map(COL_K)),
            pl.BlockSpec((tr, GDN_VW), prep_map(COL_V)),
            pl.BlockSpec((tr, LANES), prep_map(COL_TAIL)),
            pl.BlockSpec((tr, GDN_VW), lambda b, c: (row_block0 + b * nc + jnp.maximum(c - 1, 0), COL_Z)),
            pl.BlockSpec((1, SUBLANES, GDN_QKV), lambda b, c: (b, 0, 0)),
            pl.BlockSpec((1, GDN_HEADS, GDN_DK, GDN_DV), lambda b, c: (b, 0, 0, 0)),
            pl.BlockSpec((SUBLANES, GDN_QKV), const2),
            pl.BlockSpec((1, LANES), const2),
            pl.BlockSpec((1, LANES), const2),
            pl.BlockSpec((2 + N_LEVELS, CHUNK, CHUNK), lambda b, c: (0, 0, 0)),
            pl.BlockSpec((1, GDN_DV), const2),
        ],
        out_specs=[
            pl.BlockSpec((tr, GDN_VW), lambda b, c: (b * nc + jnp.maximum(c - 1, 0), 0)),
            pl.BlockSpec((1, GDN_HEADS, GDN_DK, GDN_DV), lambda b, c: (b, 0, 0, 0)),
        ],
        out_shape=[
            jax.ShapeDtypeStruct((nb * nc * tr, GDN_VW), BF16),
            jax.ShapeDtypeStruct((nb, GDN_HEADS, GDN_DK, GDN_DV), F32),
        ],
        scratch_shapes=[
            pltpu.VMEM((3, tr + SUBLANES, GDN_QK), F32),
            pltpu.VMEM((GDN_HEADS, GDN_DK, GDN_DV), F32),
            pltpu.VMEM((GDN_HEADS, 2 * CHUNK, GDN_DK), BF16),
            pltpu.VMEM((GDN_HEADS, 2 * CHUNK, GDN_DK), BF16),
            pltpu.VMEM((GDN_HEADS, CHUNK, GDN_DV), F32),
            pltpu.VMEM((GDN_HEADS, LANES), F32),
        ],
        compiler_params=pltpu.CompilerParams(
            dimension_semantics=("parallel", "arbitrary"), vmem_limit_bytes=VMEM_LIMIT),
        name="gdn",
    )(proj, proj, proj, proj, proj, halo, s0, cw, arow, dtrow, _inverse_masks(), nbw)


def _halo(hist):
    return jnp.pad(hist, ((0, 0), (SUBLANES - (CONV_W - 1), 0), (0, 0)))


def _taps(w):
    return jnp.pad(w, ((0, SUBLANES - CONV_W), (0, 0)))


def _gate_weights(rg_w, ig_w):
    eye = jnp.eye(4, dtype=F32)

    def bd(w):
        return jnp.einsum('qnij,nm->qnimj', w.reshape(4, 4, LRU_BLOCK, LRU_BLOCK), eye).reshape(
            4, 4 * LRU_BLOCK, 4 * LRU_BLOCK)

    return jnp.concatenate([bd(rg_w), bd(ig_w)], axis=2).astype(BF16)


def _lane_row(vals, lane0):
    return jnp.zeros((1, LANES), F32).at[0, lane0:lane0 + vals.shape[0]].set(vals)


def _mixer(x, l, st, p):
    state_conv_a, state_lru, state_conv_b, state_delta = st
    mix_nw = p['mix_norm'][l][None]
    cw_a, cb_a = _taps(p['conv_a_w'][l]), p['conv_a_b'][l][None]
    wg = _gate_weights(p['rg_w'][l], p['ig_w'][l])
    lru_args = (cw_a, cb_a, wg, p['rg_b'][l][None], p['ig_b'][l][None], p['lru_lambda'][l][None],
                p['norm_a'][l][None])
    cw_b = _taps(p['conv_b_w'][l])
    arow = _lane_row(p['a_log'][l], LANE_G)
    dtrow = _lane_row(p['dt_bias'][l], LANE_G)
    nbw = p['norm_b'][l][None]

    proj_p, ya_p, lru_p = _front(x, mix_nw, p['w_in'], l, *lru_args)
    ncp = TP // CHUNK
    yb_p, dl_p = _gdn(proj_p, jnp.zeros((BATCH, SUBLANES, GDN_QKV), F32),
                      jnp.zeros((BATCH, GDN_HEADS, GDN_DK, GDN_DV), F32), cw_b, arow, dtrow, nbw,
                      nb=BATCH, tr=CHUNK, nc=ncp, row_block0=0, valid=VALID_P, overlap=True)

    proj_s = _inproj(x, mix_nw, p['w_in'], l, tm=ROWS_S, row_block0=ROWS_P // ROWS_S, rows=ROWS_S)
    ya_s, lru_s = _lru(proj_s, _halo(state_conv_a[l]), state_lru[l][:, None, :], *lru_args,
                       nb=DEC_BATCH, tt=DEC_SEQ, nt=1, row_block0=0, last_tile=0, last_row=DEC_SEQ - 1)
    yb_s, dl_s = _gdn(proj_s, _halo(state_conv_b[l]), state_delta[l], cw_b, arow, dtrow, nbw,
                      nb=DEC_BATCH, tr=DEC_SEQ, nc=1, row_block0=0, valid=DEC_SEQ, overlap=False)

    x = _outproj(x, ya_p, yb_p, ya_s, yb_s, p['w_out'], l)

    pp = jnp.stack([proj_p[b * TP + VALID_P - 3:b * TP + VALID_P] for b in range(BATCH)], axis=0)
    ps = proj_s.reshape(DEC_BATCH, DEC_SEQ, N_IN_PAD)[:, DEC_SEQ - 3:]
    o2 = 2 * LRU_WIDTH
    new_p = (pp[..., :LRU_WIDTH], lru_p[:, 0], pp[..., o2:o2 + GDN_QKV], dl_p)
    new_s = (ps[..., :LRU_WIDTH], lru_s[:, 0], ps[..., o2:o2 + GDN_QKV], dl_s)
    return x, new_p, new_s


def kernel(x_prompt, x_sample, state_conv_a, state_lru, state_conv_b, state_delta, meta_tokens, ffn1_norm, ffn1_w_gate, ffn1_w_up, ffn1_w_down, mix_norm, w_in, conv_a_w, conv_a_b, rg_w, rg_b, ig_w, ig_b, lru_lambda, norm_a, conv_b_w, a_log, dt_bias, norm_b, w_out, ffn2_norm, ffn2_w_gate, ffn2_w_up, ffn2_w_down, final_norm):
    w_in_b = jnp.pad(w_in.astype(BF16), ((0, 0), (0, 0), (0, N_IN_PAD - N_IN)))
    p = dict(mix_norm=mix_norm, w_in=w_in_b, conv_a_w=conv_a_w, conv_a_b=conv_a_b, rg_w=rg_w, rg_b=rg_b,
             ig_w=ig_w, ig_b=ig_b, lru_lambda=lru_lambda, norm_a=norm_a, conv_b_w=conv_b_w, a_log=a_log,
             dt_bias=dt_bias, norm_b=norm_b, w_out=w_out.astype(BF16))
    ffn1 = (ffn1_w_gate.astype(BF16), ffn1_w_up.astype(BF16), ffn1_w_down.astype(BF16))
    ffn2 = (ffn2_w_gate.astype(BF16), ffn2_w_up.astype(BF16), ffn2_w_down.astype(BF16))
    x = _assemble(meta_tokens, x_prompt, x_sample.reshape(ROWS_S, D_MODEL))
    st = (state_conv_a, state_lru, state_conv_b, state_delta)
    fw = final_norm[None]
    news_p, news_s = [], []
    for l in range(DEPTH):
        x = _ffn(x, ffn1_norm[l][None], *ffn1, fw, l, False)
        x, new_p, new_s = _mixer(x, l, st, p)
        x = _ffn(x, ffn2_norm[l][None], *ffn2, fw, l, l == DEPTH - 1)
        news_p.append(new_p)
        news_s.append(new_s)
    y_p = jnp.stack([x[b * TP + N_META:b * TP + VALID_P] for b in range(BATCH)], axis=0)
    y_s = x[ROWS_P:].reshape(DEC_BATCH, DEC_SEQ, D_MODEL)
    stack = lambda news, i: jnp.stack([n[i] for n in news], axis=0)
    return (y_p, y_s,
            stack(news_p, 0), stack(news_p, 1), stack(news_p, 2), stack(news_p, 3),
            stack(news_s, 0), stack(news_s, 1), stack(news_s, 2), stack(news_s, 3))
```

```python
import functools

import jax
import jax.numpy as jnp
from jax import lax
from jax.experimental import pallas as pl
from jax.experimental.pallas import tpu as pltpu

F32 = jnp.float32
BF16 = jnp.bfloat16

D_MODEL = 2048
BATCH = 2
SEQ = 8192
DEPTH = 2
DEC_BATCH = 16
DEC_SEQ = 16
N_META = 16
LRU_WIDTH = 1024
LRU_BLOCKS = 16
LRU_BLOCK = 64
LRU_C = 8.0
CONV_W = 4
GDN_HEADS = 8
GDN_DK = 128
GDN_DV = 128
GDN_QK = 1024
GDN_VW = 1024
GDN_QKV = 3072
N_IN = 6160
D_FF = 5632
EPS = 1e-6

LANES = 128
SUBLANES = 8
CHUNK = 128
VALID_P = N_META + SEQ
TP = 8320
ROWS_P = BATCH * TP
ROWS_S = DEC_BATCH * DEC_SEQ
ROWS = ROWS_P + ROWS_S
N_IN_PAD = 6400
COL_XA, COL_GA, COL_Q, COL_K, COL_V, COL_Z = 0, 1, 2, 3, 4, 5
COL_TAIL = 48
LANE_BETA = 0
LANE_G = 8

TM = 768
TF = 512
TN = 1280
TT_LRU = 640
VMEM_LIMIT = 56 * 1024 * 1024


def _rms(x, w):
    return x * lax.rsqrt(jnp.mean(x * x, axis=-1, keepdims=True) + EPS) * w


def _silu(x):
    return x * jax.nn.sigmoid(x)


def _softplus(x):
    return jnp.maximum(x, 0.0) + jnp.log1p(jnp.exp(-jnp.abs(x)))


def _dot(a, b):
    return jnp.dot(a, b, preferred_element_type=F32)


def _ffn_kernel(x_ref, nw_ref, wg_ref, wu_ref, wd_ref, fw_ref, o_ref, h_ref, *, final_norm):
    j = pl.program_id(1)

    @pl.when(j == 0)
    def _():
        x = x_ref[...]
        h_ref[...] = _rms(x, nw_ref[...]).astype(BF16)
        o_ref[...] = x

    h = h_ref[...]
    g = _dot(h, wg_ref[...])
    u = _dot(h, wu_ref[...])
    a = (_silu(g) * u * 0.5).astype(BF16)
    o_ref[...] += _dot(a, wd_ref[...])

    if final_norm:
        @pl.when(j == pl.num_programs(1) - 1)
        def _():
            o_ref[...] = _rms(o_ref[...], fw_ref[...])


def _ffn(x, nw, wg, wu, wd, fw, l, final_norm):
    return pl.pallas_call(
        functools.partial(_ffn_kernel, final_norm=final_norm),
        grid=(ROWS // TM, D_FF // TF),
        in_specs=[
            pl.BlockSpec((TM, D_MODEL), lambda i, j: (i, 0)),
            pl.BlockSpec((1, D_MODEL), lambda i, j: (0, 0)),
            pl.BlockSpec((None, D_MODEL, TF), lambda i, j: (l, 0, j)),
            pl.BlockSpec((None, D_MODEL, TF), lambda i, j: (l, 0, j)),
            pl.BlockSpec((None, TF, D_MODEL), lambda i, j: (l, j, 0)),
            pl.BlockSpec((1, D_MODEL), lambda i, j: (0, 0)),
        ],
        out_specs=pl.BlockSpec((TM, D_MODEL), lambda i, j: (i, 0)),
        out_shape=jax.ShapeDtypeStruct((ROWS, D_MODEL), F32),
        scratch_shapes=[pltpu.VMEM((TM, D_MODEL), BF16)],
        compiler_params=pltpu.CompilerParams(
            dimension_semantics=("parallel", "arbitrary"), vmem_limit_bytes=VMEM_LIMIT),
        name="ffn",
    )(x, nw, wg, wu, wd, fw)


def _inproj_kernel(x_ref, nw_ref, w_ref, o_ref, h_ref):
    @pl.when(pl.program_id(1) == 0)
    def _():
        h_ref[...] = _rms(x_ref[...], nw_ref[...]).astype(BF16)

    o_ref[...] = _dot(h_ref[...], w_ref[...])


def _inproj(x, nw, w, l, *, tm, row_block0, rows):
    return pl.pallas_call(
        _inproj_kernel,
        grid=(rows // tm, N_IN_PAD // TN),
        in_specs=[
            pl.BlockSpec((tm, D_MODEL), lambda i, j: (row_block0 + i, 0)),
            pl.BlockSpec((1, D_MODEL), lambda i, j: (0, 0)),
            pl.BlockSpec((None, D_MODEL, TN), lambda i, j: (l, 0, j)),
        ],
        out_specs=pl.BlockSpec((tm, TN), lambda i, j: (i, j)),
        out_shape=jax.ShapeDtypeStruct((rows, N_IN_PAD), F32),
        scratch_shapes=[pltpu.VMEM((tm, D_MODEL), BF16)],
        compiler_params=pltpu.CompilerParams(
            dimension_semantics=("parallel", "arbitrary"), vmem_limit_bytes=VMEM_LIMIT),
        name="inproj",
    )(x, nw, w)


TAIL_P = ROWS_P - (ROWS // TM - 1) * TM


def _outproj_kernel(x_ref, yap_ref, ybp_ref, yas_ref, ybs_ref, w_ref, o_ref):
    def project(ya, yb):
        o_ref[...] = x_ref[...] + _dot(ya, w_ref[0:LRU_WIDTH, :]) + _dot(yb, w_ref[LRU_WIDTH:, :])

    last = pl.num_programs(0) - 1

    @pl.when(pl.program_id(0) < last)
    def _():
        project(yap_ref[...], ybp_ref[...])

    @pl.when(pl.program_id(0) == last)
    def _():
        project(jnp.concatenate([yap_ref[0:TAIL_P, :], yas_ref[...]], axis=0),
                jnp.concatenate([ybp_ref[0:TAIL_P, :], ybs_ref[...]], axis=0))


def _outproj(x, ya_p, yb_p, ya_s, yb_s, w, l):
    assert TAIL_P + ROWS_S == TM
    return pl.pallas_call(
        _outproj_kernel,
        grid=(ROWS // TM,),
        in_specs=[
            pl.BlockSpec((TM, D_MODEL), lambda i: (i, 0)),
            pl.BlockSpec((TM, LRU_WIDTH), lambda i: (i, 0)),
            pl.BlockSpec((TM, GDN_VW), lambda i: (i, 0)),
            pl.BlockSpec((ROWS_S, LRU_WIDTH), lambda i: (0, 0)),
            pl.BlockSpec((ROWS_S, GDN_VW), lambda i: (0, 0)),
            pl.BlockSpec((None, D_MODEL, D_MODEL), lambda i: (l, 0, 0)),
        ],
        out_specs=pl.BlockSpec((TM, D_MODEL), lambda i: (i, 0)),
        out_shape=jax.ShapeDtypeStruct((ROWS, D_MODEL), F32),
        compiler_params=pltpu.CompilerParams(
            dimension_semantics=("parallel",), vmem_limit_bytes=VMEM_LIMIT),
        name="outproj",
    )(x, ya_p, yb_p, ya_s, yb_s, w)


def _conv_from_scratch(xe_ref, cw, tt):
    width = xe_ref.shape[-1]
    groups = tt // SUBLANES + 1
    x3 = xe_ref[...].reshape(groups, SUBLANES, width)
    row = lax.broadcasted_iota(jnp.int32, (1, SUBLANES, width), 1)
    y = cw[CONV_W - 1:CONV_W][None] * x3[1:]
    for s in range(1, CONV_W):
        rot = pltpu.roll(x3, s, 1)
        y = y + cw[CONV_W - 1 - s:CONV_W - s][None] * jnp.where(row < s, rot[:-1], rot[1:])
    return y.reshape(tt, width)


def _lru_rows(xa, ga, cw_ref, cb_ref, wg_ref, rgb_ref, igb_ref, lam_ref, na_ref, xe_ref, a_ref, b_ref, hc_ref,
              tt, unrolled):
    xe_ref[SUBLANES:tt + SUBLANES, :] = xa
    xc = _conv_from_scratch(xe_ref, cw_ref[...], tt) + cb_ref[...]

    gw = 4 * LRU_BLOCK
    r_parts, i_parts = [], []
    for q in range(LRU_WIDTH // gw):
        gg = _dot(xc[:, q * gw:(q + 1) * gw].astype(BF16), wg_ref[q])
        r_parts.append(gg[:, :gw])
        i_parts.append(gg[:, gw:])
    r = jax.nn.sigmoid(jnp.concatenate(r_parts, axis=1) + rgb_ref[...])
    ig = jax.nn.sigmoid(jnp.concatenate(i_parts, axis=1) + igb_ref[...])
    log_a = (-LRU_C) * r * _softplus(-lam_ref[...])
    a = jnp.exp(log_a)
    a_ref[...] = a
    b_ref[...] = jnp.sqrt(-jnp.tanh(log_a) * (a * a + 1.0)) * ig * xc

    row = lax.broadcasted_iota(jnp.int32, (SUBLANES, LRU_WIDTH), 0)

    def group(gi, hc):
        off = gi * SUBLANES if unrolled else pl.multiple_of(gi * SUBLANES, SUBLANES)
        a8 = a_ref[pl.ds(off, SUBLANES), :]
        b8 = b_ref[pl.ds(off, SUBLANES), :]
        for k in (1, 2, 4):
            keep = row >= k
            a_prev = jnp.where(keep, pltpu.roll(a8, k, 0), 1.0)
            b_prev = jnp.where(keep, pltpu.roll(b8, k, 0), 0.0)
            b8 = a8 * b_prev + b8
            a8 = a8 * a_prev
        h8 = a8 * hc + b8
        a_ref[pl.ds(off, SUBLANES), :] = h8
        return h8[SUBLANES - 1:SUBLANES, :]

    if unrolled:
        hc = hc_ref[...]
        for gi in range(tt // SUBLANES):
            hc = group(gi, hc)
        hc_ref[...] = hc
    else:
        hc_ref[...] = lax.fori_loop(0, tt // SUBLANES, group, hc_ref[...])

    xe_ref[0:SUBLANES, :] = xe_ref[tt:tt + SUBLANES, :]
    return (_rms(a_ref[...], na_ref[...]) * jax.nn.gelu(ga, approximate=True)).astype(BF16)


def _lru_kernel(xa_ref, ga_ref, halo_ref, h0_ref, cw_ref, cb_ref, wg_ref, rgb_ref, igb_ref, lam_ref, na_ref,
                ya_ref, last_ref, xe_ref, a_ref, b_ref, hc_ref, *, tt, last_tile, last_row):
    t = pl.program_id(1)

    @pl.when(t == 0)
    def _():
        xe_ref[0:SUBLANES, :] = halo_ref[0]
        hc_ref[...] = h0_ref[0]

    ya_ref[...] = _lru_rows(xa_ref[...], ga_ref[...], cw_ref, cb_ref, wg_ref, rgb_ref, igb_ref, lam_ref, na_ref,
                            xe_ref, a_ref, b_ref, hc_ref, tt, False)

    @pl.when(t == last_tile)
    def _():
        last_ref[0] = a_ref[last_row:last_row + 1, :]


def _lru(proj, halo, h0, cw, cb, wg, rgb, igb, lam, na, *, nb, tt, nt, row_block0, last_tile, last_row):
    row_map = lambda col: (lambda b, t: (row_block0 + b * nt + t, col))
    const2 = lambda b, t: (0, 0)
    return pl.pallas_call(
        functools.partial(_lru_kernel, tt=tt, last_tile=last_tile, last_row=last_row),
        grid=(nb, nt),
        in_specs=[
            pl.BlockSpec((tt, LRU_WIDTH), row_map(COL_XA)),
            pl.BlockSpec((tt, LRU_WIDTH), row_map(COL_GA)),
            pl.BlockSpec((1, SUBLANES, LRU_WIDTH), lambda b, t: (b, 0, 0)),
            pl.BlockSpec((1, 1, LRU_WIDTH), lambda b, t: (b, 0, 0)),
            pl.BlockSpec((SUBLANES, LRU_WIDTH), const2),
            pl.BlockSpec((1, LRU_WIDTH), const2),
            pl.BlockSpec((4, 4 * LRU_BLOCK, 8 * LRU_BLOCK), lambda b, t: (0, 0, 0)),
            pl.BlockSpec((1, LRU_WIDTH), const2),
            pl.BlockSpec((1, LRU_WIDTH), const2),
            pl.BlockSpec((1, LRU_WIDTH), const2),
            pl.BlockSpec((1, LRU_WIDTH), const2),
        ],
        out_specs=[
            pl.BlockSpec((tt, LRU_WIDTH), lambda b, t: (b * nt + t, 0)),
            pl.BlockSpec((1, 1, LRU_WIDTH), lambda b, t: (b, 0, 0)),
        ],
        out_shape=[
            jax.ShapeDtypeStruct((nb * nt * tt, LRU_WIDTH), BF16),
            jax.ShapeDtypeStruct((nb, 1, LRU_WIDTH), F32),
        ],
        scratch_shapes=[
            pltpu.VMEM((tt + SUBLANES, LRU_WIDTH), F32),
            pltpu.VMEM((tt, LRU_WIDTH), F32),
            pltpu.VMEM((tt, LRU_WIDTH), F32),
            pltpu.VMEM((1, LRU_WIDTH), F32),
        ],
        compiler_params=pltpu.CompilerParams(
            dimension_semantics=("parallel", "arbitrary"), vmem_limit_bytes=VMEM_LIMIT),
        name="rglru",
    )(proj, proj, halo, h0, cw, cb, wg, rgb, igb, lam, na)


NJ = N_IN_PAD // TN
SUB = TT_LRU // NJ
NT_P = ROWS_P // TT_LRU
TILES_PER_STREAM = TP // TT_LRU
XG_COLS = 2 * LRU_WIDTH


def _front_kernel(x_ref, nw_ref, w_ref, cw_ref, cb_ref, wg_ref, rgb_ref, igb_ref, lam_ref, na_ref,
                  proj_ref, ya_ref, last_ref, h_ref, xg_ref, xe_ref, a_ref, b_ref, hc_ref):
    t = pl.program_id(0)
    j = pl.program_id(1)
    slot = t % 2

    @pl.when(j == 0)
    def _():
        h_ref[...] = _rms(x_ref[...], nw_ref[...]).astype(BF16)

    @pl.when((j == 0) & (t == 0))
    def _():
        xg_ref[...] = jnp.zeros(xg_ref.shape, F32)
        xe_ref[0:SUBLANES, :] = jnp.zeros((SUBLANES, LRU_WIDTH), F32)
        hc_ref[...] = jnp.zeros(hc_ref.shape, F32)

    @pl.when((j == 0) & ((t + TILES_PER_STREAM - 1) % TILES_PER_STREAM == 0))
    def _():
        xe_ref[0:SUBLANES, :] = jnp.zeros((SUBLANES, LRU_WIDTH), F32)
        hc_ref[...] = jnp.zeros(hc_ref.shape, F32)

    r0 = pl.multiple_of(j * SUB, SUB)
    ya_ref[...] = _lru_rows(xg_ref[1 - slot, pl.ds(r0, SUB), 0:LRU_WIDTH],
                            xg_ref[1 - slot, pl.ds(r0, SUB), LRU_WIDTH:XG_COLS],
                            cw_ref, cb_ref, wg_ref, rgb_ref, igb_ref, lam_ref, na_ref,
                            xe_ref, a_ref, b_ref, hc_ref, SUB, True)
    proj_ref[...] = _dot(h_ref[...], w_ref[...])

    @pl.when(j == 0)
    def _():
        xg_ref[slot, :, 0:TN] = proj_ref[...]

    @pl.when(j == 1)
    def _():
        xg_ref[slot, :, TN:XG_COLS] = proj_ref[:, 0:XG_COLS - TN]

    last_tile, last_row = (VALID_P - 1) // TT_LRU, (VALID_P - 1) % TT_LRU
    for b in range(BATCH):
        @pl.when((t - 1 == b * TILES_PER_STREAM + last_tile) & (j == last_row // SUB))
        def _():
            last_ref[b] = a_ref[last_row % SUB:last_row % SUB + 1, :]


def _front(x, nw, w, l, cw, cb, wg, rgb, igb, lam, na):
    assert TN < XG_COLS <= 2 * TN and TT_LRU % NJ == 0 and SUB % SUBLANES == 0
    const2 = lambda t, j: (0, 0)
    return pl.pallas_call(
        _front_kernel,
        grid=(NT_P + 1, NJ),
        in_specs=[
            pl.BlockSpec((TT_LRU, D_MODEL), lambda t, j: (jnp.minimum(t, NT_P - 1), 0)),
            pl.BlockSpec((1, D_MODEL), const2),
            pl.BlockSpec((None, D_MODEL, TN), lambda t, j: (l, 0, j)),
            pl.BlockSpec((SUBLANES, LRU_WIDTH), const2),
            pl.BlockSpec((1, LRU_WIDTH), const2),
            pl.BlockSpec((4, 4 * LRU_BLOCK, 8 * LRU_BLOCK), lambda t, j: (0, 0, 0)),
            pl.BlockSpec((1, LRU_WIDTH), const2),
            pl.BlockSpec((1, LRU_WIDTH), const2),
            pl.BlockSpec((1, LRU_WIDTH), const2),
            pl.BlockSpec((1, LRU_WIDTH), const2),
        ],
        out_specs=[
            pl.BlockSpec((TT_LRU, TN), lambda t, j: (t, j)),
            pl.BlockSpec((SUB, LRU_WIDTH), lambda t, j: (jnp.maximum((t - 1) * NJ + j, 0), 0)),
            pl.BlockSpec((BATCH, 1, LRU_WIDTH), lambda t, j: (0, 0, 0)),
        ],
        out_shape=[
            jax.ShapeDtypeStruct((ROWS_P + TT_LRU, N_IN_PAD), F32),
            jax.ShapeDtypeStruct((ROWS_P, LRU_WIDTH), BF16),
            jax.ShapeDtypeStruct((BATCH, 1, LRU_WIDTH), F32),
        ],
        scratch_shapes=[
            pltpu.VMEM((TT_LRU, D_MODEL), BF16),
            pltpu.VMEM((2, TT_LRU, XG_COLS), F32),
            pltpu.VMEM((SUB + SUBLANES, LRU_WIDTH), F32),
            pltpu.VMEM((SUB, LRU_WIDTH), F32),
            pltpu.VMEM((SUB, LRU_WIDTH), F32),
            pltpu.VMEM((1, LRU_WIDTH), F32),
        ],
        compiler_params=pltpu.CompilerParams(
            dimension_semantics=("arbitrary", "arbitrary"), vmem_limit_bytes=VMEM_LIMIT),
        name="front",
    )(x, nw, w, cw, cb, wg, rgb, igb, lam, na)


def _split_bf16(x):
    hi = x.astype(BF16)
    lo = (x - hi.astype(F32)).astype(BF16)
    return hi, lo


def _pdot(x, y):
    (xh, xl), (yh, yl) = x, y
    lhs = jnp.concatenate([xh, xl, xh], axis=1)
    rhs = jnp.concatenate([yh, yh, yl], axis=0)
    return _dot(lhs, rhs)


N_LEVELS = 4
HEAD_GROUP = 8


def _inverse_masks():
    ri = lax.broadcasted_iota(jnp.int32, (CHUNK, CHUNK), 0)
    ci = lax.broadcasted_iota(jnp.int32, (CHUNK, CHUNK), 1)
    masks = [ri == ci, (ri // SUBLANES) == (ci // SUBLANES)]
    m = SUBLANES
    while m < CHUNK:
        masks.append(((ri // (2 * m)) == (ci // (2 * m))) & ((ri // m) != (ci // m)) & (ri > ci))
        m *= 2
    return jnp.stack(masks, axis=0).astype(BF16)


def _unit_lower_inverses(a_list, mask_ref, live_rows):
    eye = mask_ref[0]
    blk = mask_ref[1]
    a_s = [_split_bf16(a) for a in a_list]
    a0_s = [(ah * blk, al * blk) for ah, al in a_s]
    p2_s = [_split_bf16(_pdot(a0, a0)) for a0 in a0_s]
    p4_s = [_split_bf16(_pdot(p2, p2)) for p2 in p2_s]
    t1_s = [_split_bf16(_pdot((eye - a0h, -a0l), (eye + p2h, p2l))) for (a0h, a0l), (p2h, p2l) in zip(a0_s, p2_s)]
    t_list = [_pdot(t1, (eye + p4h, p4l)) for t1, (p4h, p4l) in zip(t1_s, p4_s)]
    for lvl in range(N_LEVELS):
        if SUBLANES << lvl >= live_rows:
            break
        sub = mask_ref[2 + lvl]
        t_s = [_split_bf16(t) for t in t_list]
        et_s = [_split_bf16(_pdot((ah * sub, al * sub), ts)) for (ah, al), ts in zip(a_s, t_s)]
        t_list = [t - _pdot(ts, et) for t, ts, et in zip(t_list, t_s, et_s)]
    return t_list


def _pad_rows(x, rows):
    if x.shape[0] == rows:
        return x
    return jnp.concatenate([x, jnp.zeros((rows - x.shape[0], x.shape[1]), x.dtype)], axis=0)


def _gdn_recurrence(lhs1_ref, lhs2_ref, u_ref, ge_ref, z_ref, nb_ref, yb_ref, s_ref, tr):
    for h in range(GDN_HEADS):
        s = s_ref[h]
        m1 = _dot(lhs1_ref[h], s.astype(BF16))
        w = u_ref[h] - m1[:CHUNK]
        m2 = _dot(lhs2_ref[h], w.astype(BF16))
        o = (m1[CHUNK:] + m2[:CHUNK])[:tr]
        s_ref[h] = ge_ref[h:h + 1, :] * s + m2[CHUNK:]
        sl = slice(h * GDN_DV, (h + 1) * GDN_DV)
        yb_ref[:, sl] = (_rms(o, nb_ref[...]) * _silu(z_ref[:, sl])).astype(BF16)


def _gdn_kernel(q_ref, k_ref, v_ref, tail_ref, z_ref, halo_ref, s0_ref, cw_ref, arow_ref, dtrow_ref, mask_ref,
                nb_ref, yb_ref, sout_ref, xe_ref, s_ref, lhs1_ref, lhs2_ref, u_ref, ge_ref,
                *, tr, nc, valid, overlap):
    c = pl.program_id(1)
    srcs = (q_ref, k_ref, v_ref)

    @pl.when(c == 0)
    def _():
        for i in range(3):
            xe_ref[i, 0:SUBLANES, :] = halo_ref[0, :, i * GDN_QK:(i + 1) * GDN_QK]
        s_ref[...] = s0_ref[0]
        lhs1_ref[...] = jnp.zeros(lhs1_ref.shape, BF16)
        lhs2_ref[...] = jnp.zeros(lhs2_ref.shape, BF16)
        u_ref[...] = jnp.zeros(u_ref.shape, F32)
        ge_ref[...] = jnp.ones(ge_ref.shape, F32)

    @pl.when(c > 0)
    def _():
        for i in range(3):
            xe_ref[i, 0:SUBLANES, :] = xe_ref[i, tr:tr + SUBLANES, :]

    recurrence = functools.partial(_gdn_recurrence, lhs1_ref, lhs2_ref, u_ref, ge_ref, z_ref, nb_ref, yb_ref,
                                   s_ref, tr)
    prepare = functools.partial(_gdn_prepare, srcs, tail_ref, cw_ref, arow_ref, dtrow_ref, mask_ref,
                                lhs1_ref, lhs2_ref, u_ref, ge_ref, xe_ref, tr, valid - jnp.minimum(c, nc - 1) * tr)
    if overlap:
        recurrence()
        prepare()
    else:
        pl.when(c > 0)(recurrence)
        pl.when(c < nc)(prepare)

    @pl.when(c == nc)
    def _():
        sout_ref[0] = s_ref[...]


def _gdn_prepare(srcs, tail_ref, cw_ref, arow_ref, dtrow_ref, mask_ref, lhs1_ref, lhs2_ref, u_ref, ge_ref, xe_ref,
                 tr, rows_left):
    qkv = []
    for i in range(3):
        xe_ref[i, SUBLANES:tr + SUBLANES, :] = srcs[i][...]
        y = _conv_from_scratch(xe_ref.at[i], cw_ref[:, i * GDN_QK:(i + 1) * GDN_QK], tr)
        qkv.append(_pad_rows(_silu(y), CHUNK))
    q_all, k_all, v_all = qkv

    tail = tail_ref[...]
    live =(lax.broadcasted_iota(jnp.int32, (tr, LANES), 0) < rows_left).astype(F32)
    beta = _pad_rows(jax.nn.sigmoid(tail) * live, CHUNK)
    g = _pad_rows(-jnp.exp(arow_ref[...]) * _softplus(tail + dtrow_ref[...]) * live, CHUNK)

    ri = lax.broadcasted_iota(jnp.int32, (CHUNK, CHUNK), 0)
    ci = lax.broadcasted_iota(jnp.int32, (CHUNK, CHUNK), 1)
    incl = ri >= ci
    strict = ri > ci
    gc = jnp.dot(incl.astype(F32), g, precision=lax.Precision.HIGHEST, preferred_element_type=F32)
    gc_t = gc.T
    ge_ref[...] = jnp.broadcast_to(
        jnp.exp(gc_t[LANE_G:LANE_G + GDN_HEADS, CHUNK - 1:CHUNK]), (GDN_HEADS, LANES))

    for h0 in range(0, GDN_HEADS, HEAD_GROUP):
        _gdn_prepare_heads(range(h0, h0 + HEAD_GROUP), q_all, k_all, v_all, gc, gc_t, beta, incl, strict,
                           mask_ref, lhs1_ref, lhs2_ref, u_ref, tr)


def _gdn_prepare_heads(heads, q_all, k_all, v_all, gc, gc_t, beta, incl, strict, mask_ref, lhs1_ref, lhs2_ref, u_ref,
                       live_rows):
    a_list, rhs_list = [], []
    for h in heads:
        sl = slice(h * GDN_DK, (h + 1) * GDN_DK)
        qh, kh, vh = q_all[:, sl], k_all[:, sl], v_all[:, sl]
        qh = qh * lax.rsqrt(jnp.sum(qh * qh, axis=-1, keepdims=True) + EPS) * (GDN_DK ** -0.5)
        kh = kh * lax.rsqrt(jnp.sum(kh * kh, axis=-1, keepdims=True) + EPS)
        gcol = gc[:, LANE_G + h:LANE_G + h + 1]
        grow = gc_t[LANE_G + h:LANE_G + h + 1, :]
        bcol = beta[:, LANE_BETA + h:LANE_BETA + h + 1]
        decay = jnp.where(incl, jnp.exp(jnp.where(incl, gcol - grow, 0.0)), 0.0)
        kb = kh.astype(BF16)
        qkk = lax.dot_general(jnp.concatenate([qh.astype(BF16), kb], axis=0), kb,
                              (((1,), (1,)), ((), ())), preferred_element_type=F32)
        a_list.append(jnp.where(strict, bcol * decay * qkk[CHUNK:], 0.0))
        egc = jnp.exp(gcol)
        rhs_list.append(jnp.concatenate([bcol * vh, (bcol * egc) * kh], axis=1))
        lhs1_ref[h, CHUNK:, :] = (qh * egc).astype(BF16)
        k_end = kh * jnp.exp(grow[:, CHUNK - 1:CHUNK] - gcol)
        lhs2_ref[h, 0:CHUNK, :] = (qkk[:CHUNK] * decay).astype(BF16)
        lhs2_ref[h, CHUNK:, :] = k_end.T.astype(BF16)

    t_list = _unit_lower_inverses(a_list, mask_ref, live_rows)
    for i, h in enumerate(heads):
        sol = _pdot(_split_bf16(t_list[i]), _split_bf16(rhs_list[i]))
        u_ref[h] = sol[:, :GDN_DV]
        lhs1_ref[h, 0:CHUNK, :] = sol[:, GDN_DV:].astype(BF16)


def _gdn(proj, halo, s0, cw, arow, dtrow, nbw, *, nb, tr, nc, row_block0, valid, overlap):
    prep_map = lambda col: (lambda b, c: (row_block0 + b * nc + jnp.minimum(c, nc - 1), col))
    const2 = lambda b, c: (0, 0)
    return pl.pallas_call(
        functools.partial(_gdn_kernel, tr=tr, nc=nc, valid=valid, overlap=overlap),
        grid=(nb, nc + 1),
        in_specs=[
            pl.BlockSpec((tr, GDN_QK), prep_map(COL_Q)),
            pl.BlockSpec((tr, GDN_QK), prep_map(COL_K)),
            pl.BlockSpec((tr, GDN_VW), prep_map(COL_V)),
            pl.BlockSpec((tr, LANES), prep_map(COL_TAIL)),
            pl.BlockSpec((tr, GDN_VW), lambda b, c: (row_block0 + b * nc + jnp.maximum(c - 1, 0), COL_Z)),
            pl.BlockSpec((1, SUBLANES, GDN_QKV), lambda b, c: (b, 0, 0)),
            pl.BlockSpec((1, GDN_HEADS, GDN_DK, GDN_DV), lambda b, c: (b, 0, 0, 0)),
            pl.BlockSpec((SUBLANES, GDN_QKV), const2),
            pl.BlockSpec((1, LANES), const2),
            pl.BlockSpec((1, LANES), const2),
            pl.BlockSpec((2 + N_LEVELS, CHUNK, CHUNK), lambda b, c: (0, 0, 0)),
            pl.BlockSpec((1, GDN_DV), const2),
        ],
        out_specs=[
            pl.BlockSpec((tr, GDN_VW), lambda b, c: (b * nc + jnp.maximum(c - 1, 0), 0)),
            pl.BlockSpec((1, GDN_HEADS, GDN_DK, GDN_DV), lambda b, c: (b, 0, 0, 0)),
        ],
        out_shape=[
            jax.ShapeDtypeStruct((nb * nc * tr, GDN_VW), BF16),
            jax.ShapeDtypeStruct((nb, GDN_HEADS, GDN_DK, GDN_DV), F32),
        ],
        scratch_shapes=[
            pltpu.VMEM((3, tr + SUBLANES, GDN_QK), F32),
            pltpu.VMEM((GDN_HEADS, GDN_DK, GDN_DV), F32),
            pltpu.VMEM((GDN_HEADS, 2 * CHUNK, GDN_DK), BF16),
            pltpu.VMEM((GDN_HEADS, 2 * CHUNK, GDN_DK), BF16),
            pltpu.VMEM((GDN_HEADS, CHUNK, GDN_DV), F32),
            pltpu.VMEM((GDN_HEADS, LANES), F32),
        ],
        compiler_params=pltpu.CompilerParams(
            dimension_semantics=("parallel", "arbitrary"), vmem_limit_bytes=VMEM_LIMIT),
        name="gdn",
    )(proj, proj, proj, proj, proj, halo, s0, cw, arow, dtrow, _inverse_masks(), nbw)


def _halo(hist):
    return jnp.pad(hist, ((0, 0), (SUBLANES - (CONV_W - 1), 0), (0, 0)))


def _taps(w):
    return jnp.pad(w, ((0, SUBLANES - CONV_W), (0, 0)))


def _gate_weights(rg_w, ig_w):
    eye = jnp.eye(4, dtype=F32)

    def bd(w):
        return jnp.einsum('qnij,nm->qnimj', w.reshape(4, 4, LRU_BLOCK, LRU_BLOCK), eye).reshape(
            4, 4 * LRU_BLOCK, 4 * LRU_BLOCK)

    return jnp.concatenate([bd(rg_w), bd(ig_w)], axis=2).astype(BF16)


def _lane_row(vals, lane0):
    return jnp.zeros((1, LANES), F32).at[0, lane0:lane0 + vals.shape[0]].set(vals)


def _mixer(x, l, st, p):
    state_conv_a, state_lru, state_conv_b, state_delta = st
    mix_nw = p['mix_norm'][l][None]
    cw_a, cb_a = _taps(p['conv_a_w'][l]), p['conv_a_b'][l][None]
    wg = _gate_weights(p['rg_w'][l], p['ig_w'][l])
    lru_args = (cw_a, cb_a, wg, p['rg_b'][l][None], p['ig_b'][l][None], p['lru_lambda'][l][None],
                p['norm_a'][l][None])
    cw_b = _taps(p['conv_b_w'][l])
    arow = _lane_row(p['a_log'][l], LANE_G)
    dtrow = _lane_row(p['dt_bias'][l], LANE_G)
    nbw = p['norm_b'][l][None]

    proj_p, ya_p, lru_p = _front(x, mix_nw, p['w_in'], l, *lru_args)
    ncp = TP // CHUNK
    yb_p, dl_p = _gdn(proj_p, jnp.zeros((BATCH, SUBLANES, GDN_QKV), F32),
                      jnp.zeros((BATCH, GDN_HEADS, GDN_DK, GDN_DV), F32), cw_b, arow, dtrow, nbw,
                      nb=BATCH, tr=CHUNK, nc=ncp, row_block0=0, valid=VALID_P, overlap=True)

    proj_s = _inproj(x, mix_nw, p['w_in'], l, tm=ROWS_S, row_block0=ROWS_P // ROWS_S, rows=ROWS_S)
    ya_s, lru_s = _lru(proj_s, _halo(state_conv_a[l]), state_lru[l][:, None, :], *lru_args,
                       nb=DEC_BATCH, tt=DEC_SEQ, nt=1, row_block0=0, last_tile=0, last_row=DEC_SEQ - 1)
    yb_s, dl_s = _gdn(proj_s, _halo(state_conv_b[l]), state_delta[l], cw_b, arow, dtrow, nbw,
                      nb=DEC_BATCH, tr=DEC_SEQ, nc=1, row_block0=0, valid=DEC_SEQ, overlap=False)

    x = _outproj(x, ya_p, yb_p, ya_s, yb_s, p['w_out'], l)

    pp = jnp.stack([proj_p[b * TP + VALID_P - 3:b * TP + VALID_P] for b in range(BATCH)], axis=0)
    ps = proj_s.reshape(DEC_BATCH, DEC_SEQ, N_IN_PAD)[:, DEC_SEQ - 3:]
    o2 = 2 * LRU_WIDTH
    new_p = (pp[..., :LRU_WIDTH], lru_p[:, 0], pp[..., o2:o2 + GDN_QKV], dl_p)
    new_s = (ps[..., :LRU_WIDTH], lru_s[:, 0], ps[..., o2:o2 + GDN_QKV], dl_s)
    return x, new_p, new_s


def kernel(x_prompt, x_sample, state_conv_a, state_lru, state_conv_b, state_delta, meta_tokens, ffn1_norm, ffn1_w_gate, ffn1_w_up, ffn1_w_down, mix_norm, w_in, conv_a_w, conv_a_b, rg_w, rg_b, ig_w, ig_b, lru_lambda, norm_a, conv_b_w, a_log, dt_bias, norm_b, w_out, ffn2_norm, ffn2_w_gate, ffn2_w_up, ffn2_w_down, final_norm):
    w_in_b = jnp.pad(w_in.astype(BF16), ((0, 0), (0, 0), (0, N_IN_PAD - N_IN)))
    p = dict(mix_norm=mix_norm, w_in=w_in_b, conv_a_w=conv_a_w, conv_a_b=conv_a_b, rg_w=rg_w, rg_b=rg_b,
             ig_w=ig_w, ig_b=ig_b, lru_lambda=lru_lambda, norm_a=norm_a, conv_b_w=conv_b_w, a_log=a_log,
             dt_bias=dt_bias, norm_b=norm_b, w_out=w_out.astype(BF16))
    ffn1 = (ffn1_w_gate.astype(BF16), ffn1_w_up.astype(BF16), ffn1_w_down.astype(BF16))
    ffn2 = (ffn2_w_gate.astype(BF16), ffn2_w_up.astype(BF16), ffn2_w_down.astype(BF16))
    pad = jnp.zeros((TP - VALID_P, D_MODEL), F32)
    parts = []
    for b in range(BATCH):
        parts += [meta_tokens, x_prompt[b], pad]
    x = jnp.concatenate(parts + [x_sample.reshape(ROWS_S, D_MODEL)], axis=0)
    st = (state_conv_a, state_lru, state_conv_b, state_delta)
    fw = final_norm[None]
    news_p, news_s = [], []
    for l in range(DEPTH):
        x = _ffn(x, ffn1_norm[l][None], *ffn1, fw, l, False)
        x, new_p, new_s = _mixer(x, l, st, p)
        x = _ffn(x, ffn2_norm[l][None], *ffn2, fw, l, l == DEPTH - 1)
        news_p.append(new_p)
        news_s.append(new_s)
    y_p = jnp.stack([x[b * TP + N_META:b * TP + VALID_P] for b in range(BATCH)], axis=0)
    y_s = x[ROWS_P:].reshape(DEC_BATCH, DEC_SEQ, D_MODEL)
    stack = lambda news, i: jnp.stack([n[i] for n in news], axis=0)
    return (y_p, y_s,
            stack(news_p, 0), stack(news_p, 1), stack(news_p, 2), stack(news_p, 3),
            stack(news_s, 0), stack(news_s, 1), stack(news_s, 2), stack(news_s, 3))
```

```python
import functools

import jax
import jax.numpy as jnp
from jax import lax
from jax.experimental import pallas as pl
from jax.experimental.pallas import tpu as pltpu

F32 = jnp.float32
BF16 = jnp.bfloat16

D_MODEL = 2048
BATCH = 2
SEQ = 8192
DEPTH = 2
DEC_BATCH = 16
DEC_SEQ = 16
N_META = 16
LRU_WIDTH = 1024
LRU_BLOCKS = 16
LRU_BLOCK = 64
LRU_C = 8.0
CONV_W = 4
GDN_HEADS = 8
GDN_DK = 128
GDN_DV = 128
GDN_QK = 1024
GDN_VW = 1024
GDN_QKV = 3072
N_IN = 6160
D_FF = 5632
EPS = 1e-6

LANES = 128
SUBLANES = 8
CHUNK = 128
VALID_P = N_META + SEQ
TP = 8320
ROWS_P = BATCH * TP
ROWS_S = DEC_BATCH * DEC_SEQ
ROWS = ROWS_P + ROWS_S
N_IN_PAD = 6400
COL_XA, COL_GA, COL_Q, COL_K, COL_V, COL_Z = 0, 1, 2, 3, 4, 5
COL_TAIL = 48
LANE_BETA = 0
LANE_G = 8

TM = 768
TF = 512
TN = 1280
TT_LRU = 640
VMEM_LIMIT = 56 * 1024 * 1024


def _rms(x, w):
    return x * lax.rsqrt(jnp.mean(x * x, axis=-1, keepdims=True) + EPS) * w


def _silu(x):
    return x * jax.nn.sigmoid(x)


def _softplus(x):
    return jnp.maximum(x, 0.0) + jnp.log1p(jnp.exp(-jnp.abs(x)))


def _dot(a, b):
    return jnp.dot(a, b, preferred_element_type=F32)


def _ffn_kernel(x_ref, nw_ref, wg_ref, wu_ref, wd_ref, fw_ref, o_ref, h_ref, *, final_norm):
    j = pl.program_id(1)

    @pl.when(j == 0)
    def _():
        x = x_ref[...]
        h_ref[...] = _rms(x, nw_ref[...]).astype(BF16)
        o_ref[...] = x

    h = h_ref[...]
    g = _dot(h, wg_ref[...])
    u = _dot(h, wu_ref[...])
    a = (_silu(g) * u * 0.5).astype(BF16)
    o_ref[...] += _dot(a, wd_ref[...].astype(BF16))

    if final_norm:
        @pl.when(j == pl.num_programs(1) - 1)
        def _():
            o_ref[...] = _rms(o_ref[...], fw_ref[...])


def _ffn(x, nw, wg, wu, wd, fw, l, final_norm):
    return pl.pallas_call(
        functools.partial(_ffn_kernel, final_norm=final_norm),
        grid=(ROWS // TM, D_FF // TF),
        in_specs=[
            pl.BlockSpec((TM, D_MODEL), lambda i, j: (i, 0)),
            pl.BlockSpec((1, D_MODEL), lambda i, j: (0, 0)),
            pl.BlockSpec((None, D_MODEL, TF), lambda i, j: (l, 0, j)),
            pl.BlockSpec((None, D_MODEL, TF), lambda i, j: (l, 0, j)),
            pl.BlockSpec((None, TF, D_MODEL), lambda i, j: (l, j, 0)),
            pl.BlockSpec((1, D_MODEL), lambda i, j: (0, 0)),
        ],
        out_specs=pl.BlockSpec((TM, D_MODEL), lambda i, j: (i, 0)),
        out_shape=jax.ShapeDtypeStruct((ROWS, D_MODEL), F32),
        scratch_shapes=[pltpu.VMEM((TM, D_MODEL), BF16)],
        compiler_params=pltpu.CompilerParams(
            dimension_semantics=("parallel", "arbitrary"), vmem_limit_bytes=VMEM_LIMIT),
        name="ffn",
    )(x, nw, wg, wu, wd, fw)


def _inproj_kernel(x_ref, nw_ref, w_ref, o_ref, h_ref):
    @pl.when(pl.program_id(1) == 0)
    def _():
        h_ref[...] = _rms(x_ref[...], nw_ref[...]).astype(BF16)

    o_ref[...] = _dot(h_ref[...], w_ref[...])


def _inproj(x, nw, w, l, *, tm, row_block0, rows):
    return pl.pallas_call(
        _inproj_kernel,
        grid=(rows // tm, N_IN_PAD // TN),
        in_specs=[
            pl.BlockSpec((tm, D_MODEL), lambda i, j: (row_block0 + i, 0)),
            pl.BlockSpec((1, D_MODEL), lambda i, j: (0, 0)),
            pl.BlockSpec((None, D_MODEL, TN), lambda i, j: (l, 0, j)),
        ],
        out_specs=pl.BlockSpec((tm, TN), lambda i, j: (i, j)),
        out_shape=jax.ShapeDtypeStruct((rows, N_IN_PAD), F32),
        scratch_shapes=[pltpu.VMEM((tm, D_MODEL), BF16)],
        compiler_params=pltpu.CompilerParams(
            dimension_semantics=("parallel", "arbitrary"), vmem_limit_bytes=VMEM_LIMIT),
        name="inproj",
    )(x, nw, w)


TAIL_P = ROWS_P - (ROWS // TM - 1) * TM


def _outproj_kernel(x_ref, yap_ref, ybp_ref, yas_ref, ybs_ref, w_ref, o_ref):
    def project(ya, yb):
        o_ref[...] = x_ref[...] + _dot(ya, w_ref[0:LRU_WIDTH, :]) + _dot(yb, w_ref[LRU_WIDTH:, :])

    last = pl.num_programs(0) - 1

    @pl.when(pl.program_id(0) < last)
    def _():
        project(yap_ref[...], ybp_ref[...])

    @pl.when(pl.program_id(0) == last)
    def _():
        project(jnp.concatenate([yap_ref[0:TAIL_P, :], yas_ref[...]], axis=0),
                jnp.concatenate([ybp_ref[0:TAIL_P, :], ybs_ref[...]], axis=0))


def _outproj(x, ya_p, yb_p, ya_s, yb_s, w, l):
    assert TAIL_P + ROWS_S == TM
    return pl.pallas_call(
        _outproj_kernel,
        grid=(ROWS // TM,),
        in_specs=[
            pl.BlockSpec((TM, D_MODEL), lambda i: (i, 0)),
            pl.BlockSpec((TM, LRU_WIDTH), lambda i: (i, 0)),
            pl.BlockSpec((TM, GDN_VW), lambda i: (i, 0)),
            pl.BlockSpec((ROWS_S, LRU_WIDTH), lambda i: (0, 0)),
            pl.BlockSpec((ROWS_S, GDN_VW), lambda i: (0, 0)),
            pl.BlockSpec((None, D_MODEL, D_MODEL), lambda i: (l, 0, 0)),
        ],
        out_specs=pl.BlockSpec((TM, D_MODEL), lambda i: (i, 0)),
        out_shape=jax.ShapeDtypeStruct((ROWS, D_MODEL), F32),
        compiler_params=pltpu.CompilerParams(
            dimension_semantics=("parallel",), vmem_limit_bytes=VMEM_LIMIT),
        name="outproj",
    )(x, ya_p, yb_p, ya_s, yb_s, w)


def _conv_from_scratch(xe_ref, cw, tt):
    width = xe_ref.shape[-1]
    groups = tt // SUBLANES + 1
    x3 = xe_ref[...].reshape(groups, SUBLANES, width)
    row = lax.broadcasted_iota(jnp.int32, (1, SUBLANES, width), 1)
    y = cw[CONV_W - 1:CONV_W][None] * x3[1:]
    for s in range(1, CONV_W):
        rot = pltpu.roll(x3, s, 1)
        y = y + cw[CONV_W - 1 - s:CONV_W - s][None] * jnp.where(row < s, rot[:-1], rot[1:])
    return y.reshape(tt, width)


def _lru_rows(xa, ga, cw_ref, cb_ref, wg_ref, rgb_ref, igb_ref, lam_ref, na_ref, xe_ref, a_ref, b_ref, hc_ref,
              tt, unrolled):
    xe_ref[SUBLANES:tt + SUBLANES, :] = xa
    xc = _conv_from_scratch(xe_ref, cw_ref[...], tt) + cb_ref[...]

    gw = 4 * LRU_BLOCK
    r_parts, i_parts = [], []
    for q in range(LRU_WIDTH // gw):
        gg = _dot(xc[:, q * gw:(q + 1) * gw].astype(BF16), wg_ref[q])
        r_parts.append(gg[:, :gw])
        i_parts.append(gg[:, gw:])
    r = jax.nn.sigmoid(jnp.concatenate(r_parts, axis=1) + rgb_ref[...])
    ig = jax.nn.sigmoid(jnp.concatenate(i_parts, axis=1) + igb_ref[...])
    log_a = (-LRU_C) * r * _softplus(-lam_ref[...])
    a = jnp.exp(log_a)
    a_ref[...] = a
    b_ref[...] = jnp.sqrt(-jnp.tanh(log_a) * (a * a + 1.0)) * ig * xc

    row = lax.broadcasted_iota(jnp.int32, (SUBLANES, LRU_WIDTH), 0)

    def group(gi, hc):
        off = gi * SUBLANES if unrolled else pl.multiple_of(gi * SUBLANES, SUBLANES)
        a8 = a_ref[pl.ds(off, SUBLANES), :]
        b8 = b_ref[pl.ds(off, SUBLANES), :]
        for k in (1, 2, 4):
            keep = row >= k
            a_prev = jnp.where(keep, pltpu.roll(a8, k, 0), 1.0)
            b_prev = jnp.where(keep, pltpu.roll(b8, k, 0), 0.0)
            b8 = a8 * b_prev + b8
            a8 = a8 * a_prev
        h8 = a8 * hc + b8
        a_ref[pl.ds(off, SUBLANES), :] = h8
        return h8[SUBLANES - 1:SUBLANES, :]

    if unrolled:
        hc = hc_ref[...]
        for gi in range(tt // SUBLANES):
            hc = group(gi, hc)
        hc_ref[...] = hc
    else:
        hc_ref[...] = lax.fori_loop(0, tt // SUBLANES, group, hc_ref[...])

    xe_ref[0:SUBLANES, :] = xe_ref[tt:tt + SUBLANES, :]
    return (_rms(a_ref[...], na_ref[...]) * jax.nn.gelu(ga, approximate=True)).astype(BF16)


def _lru_kernel(xa_ref, ga_ref, halo_ref, h0_ref, cw_ref, cb_ref, wg_ref, rgb_ref, igb_ref, lam_ref, na_ref,
                ya_ref, last_ref, xe_ref, a_ref, b_ref, hc_ref, *, tt, last_tile, last_row):
    t = pl.program_id(1)

    @pl.when(t == 0)
    def _():
        xe_ref[0:SUBLANES, :] = halo_ref[0]
        hc_ref[...] = h0_ref[0]

    ya_ref[...] = _lru_rows(xa_ref[...], ga_ref[...], cw_ref, cb_ref, wg_ref, rgb_ref, igb_ref, lam_ref, na_ref,
                            xe_ref, a_ref, b_ref, hc_ref, tt, False)

    @pl.when(t == last_tile)
    def _():
        last_ref[0] = a_ref[last_row:last_row + 1, :]


def _lru(proj, halo, h0, cw, cb, wg, rgb, igb, lam, na, *, nb, tt, nt, row_block0, last_tile, last_row):
    row_map = lambda col: (lambda b, t: (row_block0 + b * nt + t, col))
    const2 = lambda b, t: (0, 0)
    return pl.pallas_call(
        functools.partial(_lru_kernel, tt=tt, last_tile=last_tile, last_row=last_row),
        grid=(nb, nt),
        in_specs=[
            pl.BlockSpec((tt, LRU_WIDTH), row_map(COL_XA)),
            pl.BlockSpec((tt, LRU_WIDTH), row_map(COL_GA)),
            pl.BlockSpec((1, SUBLANES, LRU_WIDTH), lambda b, t: (b, 0, 0)),
            pl.BlockSpec((1, 1, LRU_WIDTH), lambda b, t: (b, 0, 0)),
            pl.BlockSpec((SUBLANES, LRU_WIDTH), const2),
            pl.BlockSpec((1, LRU_WIDTH), const2),
            pl.BlockSpec((4, 4 * LRU_BLOCK, 8 * LRU_BLOCK), lambda b, t: (0, 0, 0)),
            pl.BlockSpec((1, LRU_WIDTH), const2),
            pl.BlockSpec((1, LRU_WIDTH), const2),
            pl.BlockSpec((1, LRU_WIDTH), const2),
            pl.BlockSpec((1, LRU_WIDTH), const2),
        ],
        out_specs=[
            pl.BlockSpec((tt, LRU_WIDTH), lambda b, t: (b * nt + t, 0)),
            pl.BlockSpec((1, 1, LRU_WIDTH), lambda b, t: (b, 0, 0)),
        ],
        out_shape=[
            jax.ShapeDtypeStruct((nb * nt * tt, LRU_WIDTH), BF16),
            jax.ShapeDtypeStruct((nb, 1, LRU_WIDTH), F32),
        ],
        scratch_shapes=[
            pltpu.VMEM((tt + SUBLANES, LRU_WIDTH), F32),
            pltpu.VMEM((tt, LRU_WIDTH), F32),
            pltpu.VMEM((tt, LRU_WIDTH), F32),
            pltpu.VMEM((1, LRU_WIDTH), F32),
        ],
        compiler_params=pltpu.CompilerParams(
            dimension_semantics=("parallel", "arbitrary"), vmem_limit_bytes=VMEM_LIMIT),
        name="rglru",
    )(proj, proj, halo, h0, cw, cb, wg, rgb, igb, lam, na)


NJ = N_IN_PAD // TN
SUB = TT_LRU // NJ
NT_P = ROWS_P // TT_LRU
TILES_PER_STREAM = TP // TT_LRU
XG_COLS = 2 * LRU_WIDTH


def _front_kernel(x_ref, nw_ref, w_ref, cw_ref, cb_ref, wg_ref, rgb_ref, igb_ref, lam_ref, na_ref,
                  proj_ref, ya_ref, last_ref, h_ref, xg_ref, xe_ref, a_ref, b_ref, hc_ref):
    t = pl.program_id(0)
    j = pl.program_id(1)
    slot = t % 2

    @pl.when(j == 0)
    def _():
        h_ref[...] = _rms(x_ref[...], nw_ref[...]).astype(BF16)

    @pl.when((j == 0) & (t == 0))
    def _():
        xg_ref[...] = jnp.zeros(xg_ref.shape, F32)
        xe_ref[0:SUBLANES, :] = jnp.zeros((SUBLANES, LRU_WIDTH), F32)
        hc_ref[...] = jnp.zeros(hc_ref.shape, F32)

    @pl.when((j == 0) & ((t + TILES_PER_STREAM - 1) % TILES_PER_STREAM == 0))
    def _():
        xe_ref[0:SUBLANES, :] = jnp.zeros((SUBLANES, LRU_WIDTH), F32)
        hc_ref[...] = jnp.zeros(hc_ref.shape, F32)

    r0 = pl.multiple_of(j * SUB, SUB)
    ya_ref[...] = _lru_rows(xg_ref[1 - slot, pl.ds(r0, SUB), 0:LRU_WIDTH],
                            xg_ref[1 - slot, pl.ds(r0, SUB), LRU_WIDTH:XG_COLS],
                            cw_ref, cb_ref, wg_ref, rgb_ref, igb_ref, lam_ref, na_ref,
                            xe_ref, a_ref, b_ref, hc_ref, SUB, True)
    proj_ref[...] = _dot(h_ref[...], w_ref[...])

    @pl.when(j == 0)
    def _():
        xg_ref[slot, :, 0:TN] = proj_ref[...]

    @pl.when(j == 1)
    def _():
        xg_ref[slot, :, TN:XG_COLS] = proj_ref[:, 0:XG_COLS - TN]

    last_tile, last_row = (VALID_P - 1) // TT_LRU, (VALID_P - 1) % TT_LRU
    for b in range(BATCH):
        @pl.when((t - 1 == b * TILES_PER_STREAM + last_tile) & (j == last_row // SUB))
        def _():
            last_ref[b] = a_ref[last_row % SUB:last_row % SUB + 1, :]


def _front(x, nw, w, l, cw, cb, wg, rgb, igb, lam, na):
    assert TN < XG_COLS <= 2 * TN and TT_LRU % NJ == 0 and SUB % SUBLANES == 0
    const2 = lambda t, j: (0, 0)
    return pl.pallas_call(
        _front_kernel,
        grid=(NT_P + 1, NJ),
        in_specs=[
            pl.BlockSpec((TT_LRU, D_MODEL), lambda t, j: (jnp.minimum(t, NT_P - 1), 0)),
            pl.BlockSpec((1, D_MODEL), const2),
            pl.BlockSpec((None, D_MODEL, TN), lambda t, j: (l, 0, j)),
            pl.BlockSpec((SUBLANES, LRU_WIDTH), const2),
            pl.BlockSpec((1, LRU_WIDTH), const2),
            pl.BlockSpec((4, 4 * LRU_BLOCK, 8 * LRU_BLOCK), lambda t, j: (0, 0, 0)),
            pl.BlockSpec((1, LRU_WIDTH), const2),
            pl.BlockSpec((1, LRU_WIDTH), const2),
            pl.BlockSpec((1, LRU_WIDTH), const2),
            pl.BlockSpec((1, LRU_WIDTH), const2),
        ],
        out_specs=[
            pl.BlockSpec((TT_LRU, TN), lambda t, j: (t, j)),
            pl.BlockSpec((SUB, LRU_WIDTH), lambda t, j: (jnp.maximum((t - 1) * NJ + j, 0), 0)),
            pl.BlockSpec((BATCH, 1, LRU_WIDTH), lambda t, j: (0, 0, 0)),
        ],
        out_shape=[
            jax.ShapeDtypeStruct((ROWS_P + TT_LRU, N_IN_PAD), F32),
            jax.ShapeDtypeStruct((ROWS_P, LRU_WIDTH), BF16),
            jax.ShapeDtypeStruct((BATCH, 1, LRU_WIDTH), F32),
        ],
        scratch_shapes=[
            pltpu.VMEM((TT_LRU, D_MODEL), BF16),
            pltpu.VMEM((2, TT_LRU, XG_COLS), F32),
            pltpu.VMEM((SUB + SUBLANES, LRU_WIDTH), F32),
            pltpu.VMEM((SUB, LRU_WIDTH), F32),
            pltpu.VMEM((SUB, LRU_WIDTH), F32),
            pltpu.VMEM((1, LRU_WIDTH), F32),
        ],
        compiler_params=pltpu.CompilerParams(
            dimension_semantics=("arbitrary", "arbitrary"), vmem_limit_bytes=VMEM_LIMIT),
        name="front",
    )(x, nw, w, cw, cb, wg, rgb, igb, lam, na)


def _split_bf16(x):
    hi = x.astype(BF16)
    lo = (x - hi.astype(F32)).astype(BF16)
    return hi, lo


def _pdot(x, y):
    (xh, xl), (yh, yl) = x, y
    lhs = jnp.concatenate([xh, xl, xh], axis=1)
    rhs = jnp.concatenate([yh, yh, yl], axis=0)
    return _dot(lhs, rhs)


N_LEVELS = 4
HEAD_GROUP = 8


def _inverse_masks():
    ri = lax.broadcasted_iota(jnp.int32, (CHUNK, CHUNK), 0)
    ci = lax.broadcasted_iota(jnp.int32, (CHUNK, CHUNK), 1)
    masks = [ri == ci, (ri // SUBLANES) == (ci // SUBLANES)]
    m = SUBLANES
    while m < CHUNK:
        masks.append(((ri // (2 * m)) == (ci // (2 * m))) & ((ri // m) != (ci // m)) & (ri > ci))
        m *= 2
    return jnp.stack(masks, axis=0).astype(BF16)


def _unit_lower_inverses(a_list, mask_ref, live_rows):
    eye = mask_ref[0]
    blk = mask_ref[1]
    a_s = [_split_bf16(a) for a in a_list]
    a0_s = [(ah * blk, al * blk) for ah, al in a_s]
    p2_s = [_split_bf16(_pdot(a0, a0)) for a0 in a0_s]
    p4_s = [_split_bf16(_pdot(p2, p2)) for p2 in p2_s]
    t1_s = [_split_bf16(_pdot((eye - a0h, -a0l), (eye + p2h, p2l))) for (a0h, a0l), (p2h, p2l) in zip(a0_s, p2_s)]
    t_list = [_pdot(t1, (eye + p4h, p4l)) for t1, (p4h, p4l) in zip(t1_s, p4_s)]
    for lvl in range(N_LEVELS):
        if SUBLANES << lvl >= live_rows:
            break
        sub = mask_ref[2 + lvl]
        t_s = [_split_bf16(t) for t in t_list]
        et_s = [_split_bf16(_pdot((ah * sub, al * sub), ts)) for (ah, al), ts in zip(a_s, t_s)]
        t_list = [t - _pdot(ts, et) for t, ts, et in zip(t_list, t_s, et_s)]
    return t_list


def _pad_rows(x, rows):
    if x.shape[0] == rows:
        return x
    return jnp.concatenate([x, jnp.zeros((rows - x.shape[0], x.shape[1]), x.dtype)], axis=0)


def _gdn_recurrence(lhs1_ref, lhs2_ref, u_ref, ge_ref, z_ref, nb_ref, yb_ref, s_ref, tr):
    for h in range(GDN_HEADS):
        s = s_ref[h]
        m1 = _dot(lhs1_ref[h], s.astype(BF16))
        w = u_ref[h] - m1[:CHUNK]
        m2 = _dot(lhs2_ref[h], w.astype(BF16))
        o = (m1[CHUNK:] + m2[:CHUNK])[:tr]
        s_ref[h] = ge_ref[h:h + 1, :] * s + m2[CHUNK:]
        sl = slice(h * GDN_DV, (h + 1) * GDN_DV)
        yb_ref[:, sl] = (_rms(o, nb_ref[...]) * _silu(z_ref[:, sl])).astype(BF16)


def _gdn_kernel(q_ref, k_ref, v_ref, tail_ref, z_ref, halo_ref, s0_ref, cw_ref, arow_ref, dtrow_ref, mask_ref,
                nb_ref, yb_ref, sout_ref, xe_ref, s_ref, lhs1_ref, lhs2_ref, u_ref, ge_ref,
                *, tr, nc, valid, overlap):
    c = pl.program_id(1)
    srcs = (q_ref, k_ref, v_ref)

    @pl.when(c == 0)
    def _():
        for i in range(3):
            xe_ref[i, 0:SUBLANES, :] = halo_ref[0, :, i * GDN_QK:(i + 1) * GDN_QK]
        s_ref[...] = s0_ref[0]
        lhs1_ref[...] = jnp.zeros(lhs1_ref.shape, BF16)
        lhs2_ref[...] = jnp.zeros(lhs2_ref.shape, BF16)
        u_ref[...] = jnp.zeros(u_ref.shape, F32)
        ge_ref[...] = jnp.ones(ge_ref.shape, F32)

    @pl.when(c > 0)
    def _():
        for i in range(3):
            xe_ref[i, 0:SUBLANES, :] = xe_ref[i, tr:tr + SUBLANES, :]

    recurrence = functools.partial(_gdn_recurrence, lhs1_ref, lhs2_ref, u_ref, ge_ref, z_ref, nb_ref, yb_ref,
                                   s_ref, tr)
    prepare = functools.partial(_gdn_prepare, srcs, tail_ref, cw_ref, arow_ref, dtrow_ref, mask_ref,
                                lhs1_ref, lhs2_ref, u_ref, ge_ref, xe_ref, tr, valid - jnp.minimum(c, nc - 1) * tr)
    if overlap:
        recurrence()
        prepare()
    else:
        pl.when(c > 0)(recurrence)
        pl.when(c < nc)(prepare)

    @pl.when(c == nc)
    def _():
        sout_ref[0] = s_ref[...]


def _gdn_prepare(srcs, tail_ref, cw_ref, arow_ref, dtrow_ref, mask_ref, lhs1_ref, lhs2_ref, u_ref, ge_ref, xe_ref,
                 tr, rows_left):
    qkv = []
    for i in range(3):
        xe_ref[i, SUBLANES:tr + SUBLANES, :] = srcs[i][...]
        y = _conv_from_scratch(xe_ref.at[i], cw_ref[:, i * GDN_QK:(i + 1) * GDN_QK], tr)
        qkv.append(_pad_rows(_silu(y), CHUNK))
    q_all, k_all, v_all = qkv

    tail = tail_ref[...]
    live =(lax.broadcasted_iota(jnp.int32, (tr, LANES), 0) < rows_left).astype(F32)
    beta = _pad_rows(jax.nn.sigmoid(tail) * live, CHUNK)
    g = _pad_rows(-jnp.exp(arow_ref[...]) * _softplus(tail + dtrow_ref[...]) * live, CHUNK)

    ri = lax.broadcasted_iota(jnp.int32, (CHUNK, CHUNK), 0)
    ci = lax.broadcasted_iota(jnp.int32, (CHUNK, CHUNK), 1)
    incl = ri >= ci
    strict = ri > ci
    gc = jnp.dot(incl.astype(F32), g, precision=lax.Precision.HIGHEST, preferred_element_type=F32)
    gc_t = gc.T
    ge_ref[...] = jnp.broadcast_to(
        jnp.exp(gc_t[LANE_G:LANE_G + GDN_HEADS, CHUNK - 1:CHUNK]), (GDN_HEADS, LANES))

    for h0 in range(0, GDN_HEADS, HEAD_GROUP):
        _gdn_prepare_heads(range(h0, h0 + HEAD_GROUP), q_all, k_all, v_all, gc, gc_t, beta, incl, strict,
                           mask_ref, lhs1_ref, lhs2_ref, u_ref, tr)


def _gdn_prepare_heads(heads, q_all, k_all, v_all, gc, gc_t, beta, incl, strict, mask_ref, lhs1_ref, lhs2_ref, u_ref,
                       live_rows):
    a_list, rhs_list = [], []
    for h in heads:
        sl = slice(h * GDN_DK, (h + 1) * GDN_DK)
        qh, kh, vh = q_all[:, sl], k_all[:, sl], v_all[:, sl]
        qh = qh * lax.rsqrt(jnp.sum(qh * qh, axis=-1, keepdims=True) + EPS) * (GDN_DK ** -0.5)
        kh = kh * lax.rsqrt(jnp.sum(kh * kh, axis=-1, keepdims=True) + EPS)
        gcol = gc[:, LANE_G + h:LANE_G + h + 1]
        grow = gc_t[LANE_G + h:LANE_G + h + 1, :]
        bcol = beta[:, LANE_BETA + h:LANE_BETA + h + 1]
        decay = jnp.where(incl, jnp.exp(jnp.where(incl, gcol - grow, 0.0)), 0.0)
        kb = kh.astype(BF16)
        qkk = lax.dot_general(jnp.concatenate([qh.astype(BF16), kb], axis=0), kb,
                              (((1,), (1,)), ((), ())), preferred_element_type=F32)
        a_list.append(jnp.where(strict, bcol * decay * qkk[CHUNK:], 0.0))
        egc = jnp.exp(gcol)
        rhs_list.append(jnp.concatenate([bcol * vh, (bcol * egc) * kh], axis=1))
        lhs1_ref[h, CHUNK:, :] = (qh * egc).astype(BF16)
        k_end = kh * jnp.exp(grow[:, CHUNK - 1:CHUNK] - gcol)
        lhs2_ref[h, 0:CHUNK, :] = (qkk[:CHUNK] * decay).astype(BF16)
        lhs2_ref[h, CHUNK:, :] = k_end.T.astype(BF16)

    t_list = _unit_lower_inverses(a_list, mask_ref, live_rows)
    for i, h in enumerate(heads):
        sol = _pdot(_split_bf16(t_list[i]), _split_bf16(rhs_list[i]))
        u_ref[h] = sol[:, :GDN_DV]
        lhs1_ref[h, 0:CHUNK, :] = sol[:, GDN_DV:].astype(BF16)


def _gdn(proj, halo, s0, cw, arow, dtrow, nbw, *, nb, tr, nc, row_block0, valid, overlap):
    prep_map = lambda col: (lambda b, c: (row_block0 + b * nc + jnp.minimum(c, nc - 1), col))
    const2 = lambda b, c: (0, 0)
    return pl.pallas_call(
        functools.partial(_gdn_kernel, tr=tr, nc=nc, valid=valid, overlap=overlap),
        grid=(nb, nc + 1),
        in_specs=[
            pl.BlockSpec((tr, GDN_QK), prep_map(COL_Q)),
            pl.BlockSpec((tr, GDN_QK), prep_map(COL_K)),
            pl.BlockSpec((tr, GDN_VW), prep_map(COL_V)),
            pl.BlockSpec((tr, LANES), prep_map(COL_TAIL)),
            pl.BlockSpec((tr, GDN_VW), lambda b, c: (row_block0 + b * nc + jnp.maximum(c - 1, 0), COL_Z)),
            pl.BlockSpec((1, SUBLANES, GDN_QKV), lambda b, c: (b, 0, 0)),
            pl.BlockSpec((1, GDN_HEADS, GDN_DK, GDN_DV), lambda b, c: (b, 0, 0, 0)),
            pl.BlockSpec((SUBLANES, GDN_QKV), const2),
            pl.BlockSpec((1, LANES), const2),
            pl.BlockSpec((1, LANES), const2),
            pl.BlockSpec((2 + N_LEVELS, CHUNK, CHUNK), lambda b, c: (0, 0, 0)),
            pl.BlockSpec((1, GDN_DV), const2),
        ],
        out_specs=[
            pl.BlockSpec((tr, GDN_VW), lambda b, c: (b * nc + jnp.maximum(c - 1, 0), 0)),
            pl.BlockSpec((1, GDN_HEADS, GDN_DK, GDN_DV), lambda b, c: (b, 0, 0, 0)),
        ],
        out_shape=[
            jax.ShapeDtypeStruct((nb * nc * tr, GDN_VW), BF16),
            jax.ShapeDtypeStruct((nb, GDN_HEADS, GDN_DK, GDN_DV), F32),
        ],
        scratch_shapes=[
            pltpu.VMEM((3, tr + SUBLANES, GDN_QK), F32),
            pltpu.VMEM((GDN_HEADS, GDN_DK, GDN_DV), F32),
            pltpu.VMEM((GDN_HEADS, 2 * CHUNK, GDN_DK), BF16),
            pltpu.VMEM((GDN_HEADS, 2 * CHUNK, GDN_DK), BF16),
            pltpu.VMEM((GDN_HEADS, CHUNK, GDN_DV), F32),
            pltpu.VMEM((GDN_HEADS, LANES), F32),
        ],
        compiler_params=pltpu.CompilerParams(
            dimension_semantics=("parallel", "arbitrary"), vmem_limit_bytes=VMEM_LIMIT),
        name="gdn",
    )(proj, proj, proj, proj, proj, halo, s0, cw, arow, dtrow, _inverse_masks(), nbw)


def _halo(hist):
    return jnp.pad(hist, ((0, 0), (SUBLANES - (CONV_W - 1), 0), (0, 0)))


def _taps(w):
    return jnp.pad(w, ((0, SUBLANES - CONV_W), (0, 0)))


def _gate_weights(rg_w, ig_w):
    eye = jnp.eye(4, dtype=F32)

    def bd(w):
        return jnp.einsum('qnij,nm->qnimj', w.reshape(4, 4, LRU_BLOCK, LRU_BLOCK), eye).reshape(
            4, 4 * LRU_BLOCK, 4 * LRU_BLOCK)

    return jnp.concatenate([bd(rg_w), bd(ig_w)], axis=2).astype(BF16)


def _lane_row(vals, lane0):
    return jnp.zeros((1, LANES), F32).at[0, lane0:lane0 + vals.shape[0]].set(vals)


def _mixer(x, l, st, p):
    state_conv_a, state_lru, state_conv_b, state_delta = st
    mix_nw = p['mix_norm'][l][None]
    cw_a, cb_a = _taps(p['conv_a_w'][l]), p['conv_a_b'][l][None]
    wg = _gate_weights(p['rg_w'][l], p['ig_w'][l])
    lru_args = (cw_a, cb_a, wg, p['rg_b'][l][None], p['ig_b'][l][None], p['lru_lambda'][l][None],
                p['norm_a'][l][None])
    cw_b = _taps(p['conv_b_w'][l])
    arow = _lane_row(p['a_log'][l], LANE_G)
    dtrow = _lane_row(p['dt_bias'][l], LANE_G)
    nbw = p['norm_b'][l][None]

    proj_p, ya_p, lru_p = _front(x, mix_nw, p['w_in'], l, *lru_args)
    ncp = TP // CHUNK
    yb_p, dl_p = _gdn(proj_p, jnp.zeros((BATCH, SUBLANES, GDN_QKV), F32),
                      jnp.zeros((BATCH, GDN_HEADS, GDN_DK, GDN_DV), F32), cw_b, arow, dtrow, nbw,
                      nb=BATCH, tr=CHUNK, nc=ncp, row_block0=0, valid=VALID_P, overlap=True)

    proj_s = _inproj(x, mix_nw, p['w_in'], l, tm=ROWS_S, row_block0=ROWS_P // ROWS_S, rows=ROWS_S)
    ya_s, lru_s = _lru(proj_s, _halo(state_conv_a[l]), state_lru[l][:, None, :], *lru_args,
                       nb=DEC_BATCH, tt=DEC_SEQ, nt=1, row_block0=0, last_tile=0, last_row=DEC_SEQ - 1)
    yb_s, dl_s = _gdn(proj_s, _halo(state_conv_b[l]), state_delta[l], cw_b, arow, dtrow, nbw,
                      nb=DEC_BATCH, tr=DEC_SEQ, nc=1, row_block0=0, valid=DEC_SEQ, overlap=False)

    x = _outproj(x, ya_p, yb_p, ya_s, yb_s, p['w_out'], l)

    pp = jnp.stack([proj_p[b * TP + VALID_P - 3:b * TP + VALID_P] for b in range(BATCH)], axis=0)
    ps = proj_s.reshape(DEC_BATCH, DEC_SEQ, N_IN_PAD)[:, DEC_SEQ - 3:]
    o2 = 2 * LRU_WIDTH
    new_p = (pp[..., :LRU_WIDTH], lru_p[:, 0], pp[..., o2:o2 + GDN_QKV], dl_p)
    new_s = (ps[..., :LRU_WIDTH], lru_s[:, 0], ps[..., o2:o2 + GDN_QKV], dl_s)
    return x, new_p, new_s


def kernel(x_prompt, x_sample, state_conv_a, state_lru, state_conv_b, state_delta, meta_tokens, ffn1_norm, ffn1_w_gate, ffn1_w_up, ffn1_w_down, mix_norm, w_in, conv_a_w, conv_a_b, rg_w, rg_b, ig_w, ig_b, lru_lambda, norm_a, conv_b_w, a_log, dt_bias, norm_b, w_out, ffn2_norm, ffn2_w_gate, ffn2_w_up, ffn2_w_down, final_norm):
    w_in_b = jnp.pad(w_in.astype(BF16), ((0, 0), (0, 0), (0, N_IN_PAD - N_IN)))
    p = dict(mix_norm=mix_norm, w_in=w_in_b, conv_a_w=conv_a_w, conv_a_b=conv_a_b, rg_w=rg_w, rg_b=rg_b,
             ig_w=ig_w, ig_b=ig_b, lru_lambda=lru_lambda, norm_a=norm_a, conv_b_w=conv_b_w, a_log=a_log,
             dt_bias=dt_bias, norm_b=norm_b, w_out=w_out.astype(BF16))
    ffn1 = (ffn1_w_gate.astype(BF16), ffn1_w_up.astype(BF16), ffn1_w_down)
    ffn2 = (ffn2_w_gate.astype(BF16), ffn2_w_up.astype(BF16), ffn2_w_down)
    pad = jnp.zeros((TP - VALID_P, D_MODEL), F32)
    parts = []
    for b in range(BATCH):
        parts += [meta_tokens, x_prompt[b], pad]
    x = jnp.concatenate(parts + [x_sample.reshape(ROWS_S, D_MODEL)], axis=0)
    st = (state_conv_a, state_lru, state_conv_b, state_delta)
    fw = final_norm[None]
    news_p, news_s = [], []
    for l in range(DEPTH):
        x = _ffn(x, ffn1_norm[l][None], *ffn1, fw, l, False)
        x, new_p, new_s = _mixer(x, l, st, p)
        x = _ffn(x, ffn2_norm[l][None], *ffn2, fw, l, l == DEPTH - 1)
        news_p.append(new_p)
        news_s.append(new_s)
    y_p = jnp.stack([x[b * TP + N_META:b * TP + VALID_P] for b in range(BATCH)], axis=0)
    y_s = x[ROWS_P:].reshape(DEC_BATCH, DEC_SEQ, D_MODEL)
    stack = lambda news, i: jnp.stack([n[i] for n in news], axis=0)
    return (y_p, y_s,
            stack(news_p, 0), stack(news_p, 1), stack(news_p, 2), stack(news_p, 3),
            stack(news_s, 0), stack(news_s, 1), stack(news_s, 2), stack(news_s, 3))
```

```python
import functools

import jax
import jax.numpy as jnp
from jax import lax
from jax.experimental import pallas as pl
from jax.experimental.pallas import tpu as pltpu

F32 = jnp.float32
BF16 = jnp.bfloat16

D_MODEL = 2048
BATCH = 2
SEQ = 8192
DEPTH = 2
DEC_BATCH = 16
DEC_SEQ = 16
N_META = 16
LRU_WIDTH = 1024
LRU_BLOCKS = 16
LRU_BLOCK = 64
LRU_C = 8.0
CONV_W = 4
GDN_HEADS = 8
GDN_DK = 128
GDN_DV = 128
GDN_QK = 1024
GDN_VW = 1024
GDN_QKV = 3072
N_IN = 6160
D_FF = 5632
EPS = 1e-6

LANES = 128
SUBLANES = 8
CHUNK = 128
VALID_P = N_META + SEQ
TP = 8320
ROWS_P = BATCH * TP
ROWS_S = DEC_BATCH * DEC_SEQ
ROWS = ROWS_P + ROWS_S
N_IN_PAD = 6400
COL_XA, COL_GA, COL_Q, COL_K, COL_V, COL_Z = 0, 1, 2, 3, 4, 5
COL_TAIL = 48
LANE_BETA = 0
LANE_G = 8

TM = 768
TF = 512
TN = 1280
TT_LRU = 640
VMEM_LIMIT = 56 * 1024 * 1024


def _rms(x, w):
    return x * lax.rsqrt(jnp.mean(x * x, axis=-1, keepdims=True) + EPS) * w


def _silu(x):
    return x * jax.nn.sigmoid(x)


def _softplus(x):
    return jnp.maximum(x, 0.0) + jnp.log1p(jnp.exp(-jnp.abs(x)))


def _dot(a, b):
    return jnp.dot(a, b, preferred_element_type=F32)


def _ffn_kernel(x_ref, nw_ref, wg_ref, wu_ref, wd_ref, fw_ref, o_ref, h_ref, *, final_norm):
    j = pl.program_id(1)

    @pl.when(j == 0)
    def _():
        x = x_ref[...]
        h_ref[...] = _rms(x, nw_ref[...]).astype(BF16)
        o_ref[...] = x

    h = h_ref[...]
    g = _dot(h, wg_ref[...].astype(BF16))
    u = _dot(h, wu_ref[...])
    a = (_silu(g) * u * 0.5).astype(BF16)
    o_ref[...] += _dot(a, wd_ref[...].astype(BF16))

    if final_norm:
        @pl.when(j == pl.num_programs(1) - 1)
        def _():
            o_ref[...] = _rms(o_ref[...], fw_ref[...])


def _ffn(x, nw, wg, wu, wd, fw, l, final_norm):
    return pl.pallas_call(
        functools.partial(_ffn_kernel, final_norm=final_norm),
        grid=(ROWS // TM, D_FF // TF),
        in_specs=[
            pl.BlockSpec((TM, D_MODEL), lambda i, j: (i, 0)),
            pl.BlockSpec((1, D_MODEL), lambda i, j: (0, 0)),
            pl.BlockSpec((None, D_MODEL, TF), lambda i, j: (l, 0, j)),
            pl.BlockSpec((None, D_MODEL, TF), lambda i, j: (l, 0, j)),
            pl.BlockSpec((None, TF, D_MODEL), lambda i, j: (l, j, 0)),
            pl.BlockSpec((1, D_MODEL), lambda i, j: (0, 0)),
        ],
        out_specs=pl.BlockSpec((TM, D_MODEL), lambda i, j: (i, 0)),
        out_shape=jax.ShapeDtypeStruct((ROWS, D_MODEL), F32),
        scratch_shapes=[pltpu.VMEM((TM, D_MODEL), BF16)],
        compiler_params=pltpu.CompilerParams(
            dimension_semantics=("parallel", "arbitrary"), vmem_limit_bytes=VMEM_LIMIT),
        name="ffn",
    )(x, nw, wg, wu, wd, fw)


def _inproj_kernel(x_ref, nw_ref, w_ref, o_ref, h_ref):
    @pl.when(pl.program_id(1) == 0)
    def _():
        h_ref[...] = _rms(x_ref[...], nw_ref[...]).astype(BF16)

    o_ref[...] = _dot(h_ref[...], w_ref[...])


def _inproj(x, nw, w, l, *, tm, row_block0, rows):
    return pl.pallas_call(
        _inproj_kernel,
        grid=(rows // tm, N_IN_PAD // TN),
        in_specs=[
            pl.BlockSpec((tm, D_MODEL), lambda i, j: (row_block0 + i, 0)),
            pl.BlockSpec((1, D_MODEL), lambda i, j: (0, 0)),
            pl.BlockSpec((None, D_MODEL, TN), lambda i, j: (l, 0, j)),
        ],
        out_specs=pl.BlockSpec((tm, TN), lambda i, j: (i, j)),
        out_shape=jax.ShapeDtypeStruct((rows, N_IN_PAD), F32),
        scratch_shapes=[pltpu.VMEM((tm, D_MODEL), BF16)],
        compiler_params=pltpu.CompilerParams(
            dimension_semantics=("parallel", "arbitrary"), vmem_limit_bytes=VMEM_LIMIT),
        name="inproj",
    )(x, nw, w)


TAIL_P = ROWS_P - (ROWS // TM - 1) * TM


def _outproj_kernel(x_ref, yap_ref, ybp_ref, yas_ref, ybs_ref, w_ref, o_ref):
    def project(ya, yb):
        o_ref[...] = x_ref[...] + _dot(ya, w_ref[0:LRU_WIDTH, :]) + _dot(yb, w_ref[LRU_WIDTH:, :])

    last = pl.num_programs(0) - 1

    @pl.when(pl.program_id(0) < last)
    def _():
        project(yap_ref[...], ybp_ref[...])

    @pl.when(pl.program_id(0) == last)
    def _():
        project(jnp.concatenate([yap_ref[0:TAIL_P, :], yas_ref[...]], axis=0),
                jnp.concatenate([ybp_ref[0:TAIL_P, :], ybs_ref[...]], axis=0))


def _outproj(x, ya_p, yb_p, ya_s, yb_s, w, l):
    assert TAIL_P + ROWS_S == TM
    return pl.pallas_call(
        _outproj_kernel,
        grid=(ROWS // TM,),
        in_specs=[
            pl.BlockSpec((TM, D_MODEL), lambda i: (i, 0)),
            pl.BlockSpec((TM, LRU_WIDTH), lambda i: (i, 0)),
            pl.BlockSpec((TM, GDN_VW), lambda i: (i, 0)),
            pl.BlockSpec((ROWS_S, LRU_WIDTH), lambda i: (0, 0)),
            pl.BlockSpec((ROWS_S, GDN_VW), lambda i: (0, 0)),
            pl.BlockSpec((None, D_MODEL, D_MODEL), lambda i: (l, 0, 0)),
        ],
        out_specs=pl.BlockSpec((TM, D_MODEL), lambda i: (i, 0)),
        out_shape=jax.ShapeDtypeStruct((ROWS, D_MODEL), F32),
        compiler_params=pltpu.CompilerParams(
            dimension_semantics=("parallel",), vmem_limit_bytes=VMEM_LIMIT),
        name="outproj",
    )(x, ya_p, yb_p, ya_s, yb_s, w)


def _conv_from_scratch(xe_ref, cw, tt):
    width = xe_ref.shape[-1]
    groups = tt // SUBLANES + 1
    x3 = xe_ref[...].reshape(groups, SUBLANES, width)
    row = lax.broadcasted_iota(jnp.int32, (1, SUBLANES, width), 1)
    y = cw[CONV_W - 1:CONV_W][None] * x3[1:]
    for s in range(1, CONV_W):
        rot = pltpu.roll(x3, s, 1)
        y = y + cw[CONV_W - 1 - s:CONV_W - s][None] * jnp.where(row < s, rot[:-1], rot[1:])
    return y.reshape(tt, width)


def _lru_rows(xa, ga, cw_ref, cb_ref, wg_ref, rgb_ref, igb_ref, lam_ref, na_ref, xe_ref, a_ref, b_ref, hc_ref,
              tt, unrolled):
    xe_ref[SUBLANES:tt + SUBLANES, :] = xa
    xc = _conv_from_scratch(xe_ref, cw_ref[...], tt) + cb_ref[...]

    gw = 4 * LRU_BLOCK
    r_parts, i_parts = [], []
    for q in range(LRU_WIDTH // gw):
        gg = _dot(xc[:, q * gw:(q + 1) * gw].astype(BF16), wg_ref[q])
        r_parts.append(gg[:, :gw])
        i_parts.append(gg[:, gw:])
    r = jax.nn.sigmoid(jnp.concatenate(r_parts, axis=1) + rgb_ref[...])
    ig = jax.nn.sigmoid(jnp.concatenate(i_parts, axis=1) + igb_ref[...])
    log_a = (-LRU_C) * r * _softplus(-lam_ref[...])
    a = jnp.exp(log_a)
    a_ref[...] = a
    b_ref[...] = jnp.sqrt(-jnp.tanh(log_a) * (a * a + 1.0)) * ig * xc

    row = lax.broadcasted_iota(jnp.int32, (SUBLANES, LRU_WIDTH), 0)

    def group(gi, hc):
        off = gi * SUBLANES if unrolled else pl.multiple_of(gi * SUBLANES, SUBLANES)
        a8 = a_ref[pl.ds(off, SUBLANES), :]
        b8 = b_ref[pl.ds(off, SUBLANES), :]
        for k in (1, 2, 4):
            keep = row >= k
            a_prev = jnp.where(keep, pltpu.roll(a8, k, 0), 1.0)
            b_prev = jnp.where(keep, pltpu.roll(b8, k, 0), 0.0)
            b8 = a8 * b_prev + b8
            a8 = a8 * a_prev
        h8 = a8 * hc + b8
        a_ref[pl.ds(off, SUBLANES), :] = h8
        return h8[SUBLANES - 1:SUBLANES, :]

    if unrolled:
        hc = hc_ref[...]
        for gi in range(tt // SUBLANES):
            hc = group(gi, hc)
        hc_ref[...] = hc
    else:
        hc_ref[...] = lax.fori_loop(0, tt // SUBLANES, group, hc_ref[...])

    xe_ref[0:SUBLANES, :] = xe_ref[tt:tt + SUBLANES, :]
    return (_rms(a_ref[...], na_ref[...]) * jax.nn.gelu(ga, approximate=True)).astype(BF16)


def _lru_kernel(xa_ref, ga_ref, halo_ref, h0_ref, cw_ref, cb_ref, wg_ref, rgb_ref, igb_ref, lam_ref, na_ref,
                ya_ref, last_ref, xe_ref, a_ref, b_ref, hc_ref, *, tt, last_tile, last_row):
    t = pl.program_id(1)

    @pl.when(t == 0)
    def _():
        xe_ref[0:SUBLANES, :] = halo_ref[0]
        hc_ref[...] = h0_ref[0]

    ya_ref[...] = _lru_rows(xa_ref[...], ga_ref[...], cw_ref, cb_ref, wg_ref, rgb_ref, igb_ref, lam_ref, na_ref,
                            xe_ref, a_ref, b_ref, hc_ref, tt, False)

    @pl.when(t == last_tile)
    def _():
        last_ref[0] = a_ref[last_row:last_row + 1, :]


def _lru(proj, halo, h0, cw, cb, wg, rgb, igb, lam, na, *, nb, tt, nt, row_block0, last_tile, last_row):
    row_map = lambda col: (lambda b, t: (row_block0 + b * nt + t, col))
    const2 = lambda b, t: (0, 0)
    return pl.pallas_call(
        functools.partial(_lru_kernel, tt=tt, last_tile=last_tile, last_row=last_row),
        grid=(nb, nt),
        in_specs=[
            pl.BlockSpec((tt, LRU_WIDTH), row_map(COL_XA)),
            pl.BlockSpec((tt, LRU_WIDTH), row_map(COL_GA)),
            pl.BlockSpec((1, SUBLANES, LRU_WIDTH), lambda b, t: (b, 0, 0)),
            pl.BlockSpec((1, 1, LRU_WIDTH), lambda b, t: (b, 0, 0)),
            pl.BlockSpec((SUBLANES, LRU_WIDTH), const2),
            pl.BlockSpec((1, LRU_WIDTH), const2),
            pl.BlockSpec((4, 4 * LRU_BLOCK, 8 * LRU_BLOCK), lambda b, t: (0, 0, 0)),
            pl.BlockSpec((1, LRU_WIDTH), const2),
            pl.BlockSpec((1, LRU_WIDTH), const2),
            pl.BlockSpec((1, LRU_WIDTH), const2),
            pl.BlockSpec((1, LRU_WIDTH), const2),
        ],
        out_specs=[
            pl.BlockSpec((tt, LRU_WIDTH), lambda b, t: (b * nt + t, 0)),
            pl.BlockSpec((1, 1, LRU_WIDTH), lambda b, t: (b, 0, 0)),
        ],
        out_shape=[
            jax.ShapeDtypeStruct((nb * nt * tt, LRU_WIDTH), BF16),
            jax.ShapeDtypeStruct((nb, 1, LRU_WIDTH), F32),
        ],
        scratch_shapes=[
            pltpu.VMEM((tt + SUBLANES, LRU_WIDTH), F32),
            pltpu.VMEM((tt, LRU_WIDTH), F32),
            pltpu.VMEM((tt, LRU_WIDTH), F32),
            pltpu.VMEM((1, LRU_WIDTH), F32),
        ],
        compiler_params=pltpu.CompilerParams(
            dimension_semantics=("parallel", "arbitrary"), vmem_limit_bytes=VMEM_LIMIT),
        name="rglru",
    )(proj, proj, halo, h0, cw, cb, wg, rgb, igb, lam, na)


NJ = N_IN_PAD // TN
SUB = TT_LRU // NJ
NT_P = ROWS_P // TT_LRU
TILES_PER_STREAM = TP // TT_LRU
XG_COLS = 2 * LRU_WIDTH


def _front_kernel(x_ref, nw_ref, w_ref, cw_ref, cb_ref, wg_ref, rgb_ref, igb_ref, lam_ref, na_ref,
                  proj_ref, ya_ref, last_ref, h_ref, xg_ref, xe_ref, a_ref, b_ref, hc_ref):
    t = pl.program_id(0)
    j = pl.program_id(1)
    slot = t % 2

    @pl.when(j == 0)
    def _():
        h_ref[...] = _rms(x_ref[...], nw_ref[...]).astype(BF16)

    @pl.when((j == 0) & (t == 0))
    def _():
        xg_ref[...] = jnp.zeros(xg_ref.shape, F32)
        xe_ref[0:SUBLANES, :] = jnp.zeros((SUBLANES, LRU_WIDTH), F32)
        hc_ref[...] = jnp.zeros(hc_ref.shape, F32)

    @pl.when((j == 0) & ((t + TILES_PER_STREAM - 1) % TILES_PER_STREAM == 0))
    def _():
        xe_ref[0:SUBLANES, :] = jnp.zeros((SUBLANES, LRU_WIDTH), F32)
        hc_ref[...] = jnp.zeros(hc_ref.shape, F32)

    r0 = pl.multiple_of(j * SUB, SUB)
    ya_ref[...] = _lru_rows(xg_ref[1 - slot, pl.ds(r0, SUB), 0:LRU_WIDTH],
                            xg_ref[1 - slot, pl.ds(r0, SUB), LRU_WIDTH:XG_COLS],
                            cw_ref, cb_ref, wg_ref, rgb_ref, igb_ref, lam_ref, na_ref,
                            xe_ref, a_ref, b_ref, hc_ref, SUB, True)
    proj_ref[...] = _dot(h_ref[...], w_ref[...])

    @pl.when(j == 0)
    def _():
        xg_ref[slot, :, 0:TN] = proj_ref[...]

    @pl.when(j == 1)
    def _():
        xg_ref[slot, :, TN:XG_COLS] = proj_ref[:, 0:XG_COLS - TN]

    last_tile, last_row = (VALID_P - 1) // TT_LRU, (VALID_P - 1) % TT_LRU
    for b in range(BATCH):
        @pl.when((t - 1 == b * TILES_PER_STREAM + last_tile) & (j == last_row // SUB))
        def _():
            last_ref[b] = a_ref[last_row % SUB:last_row % SUB + 1, :]


def _front(x, nw, w, l, cw, cb, wg, rgb, igb, lam, na):
    assert TN < XG_COLS <= 2 * TN and TT_LRU % NJ == 0 and SUB % SUBLANES == 0
    const2 = lambda t, j: (0, 0)
    return pl.pallas_call(
        _front_kernel,
        grid=(NT_P + 1, NJ),
        in_specs=[
            pl.BlockSpec((TT_LRU, D_MODEL), lambda t, j: (jnp.minimum(t, NT_P - 1), 0)),
            pl.BlockSpec((1, D_MODEL), const2),
            pl.BlockSpec((None, D_MODEL, TN), lambda t, j: (l, 0, j)),
            pl.BlockSpec((SUBLANES, LRU_WIDTH), const2),
            pl.BlockSpec((1, LRU_WIDTH), const2),
            pl.BlockSpec((4, 4 * LRU_BLOCK, 8 * LRU_BLOCK), lambda t, j: (0, 0, 0)),
            pl.BlockSpec((1, LRU_WIDTH), const2),
            pl.BlockSpec((1, LRU_WIDTH), const2),
            pl.BlockSpec((1, LRU_WIDTH), const2),
            pl.BlockSpec((1, LRU_WIDTH), const2),
        ],
        out_specs=[
            pl.BlockSpec((TT_LRU, TN), lambda t, j: (t, j)),
            pl.BlockSpec((SUB, LRU_WIDTH), lambda t, j: (jnp.maximum((t - 1) * NJ + j, 0), 0)),
            pl.BlockSpec((BATCH, 1, LRU_WIDTH), lambda t, j: (0, 0, 0)),
        ],
        out_shape=[
            jax.ShapeDtypeStruct((ROWS_P + TT_LRU, N_IN_PAD), F32),
            jax.ShapeDtypeStruct((ROWS_P, LRU_WIDTH), BF16),
            jax.ShapeDtypeStruct((BATCH, 1, LRU_WIDTH), F32),
        ],
        scratch_shapes=[
            pltpu.VMEM((TT_LRU, D_MODEL), BF16),
            pltpu.VMEM((2, TT_LRU, XG_COLS), F32),
            pltpu.VMEM((SUB + SUBLANES, LRU_WIDTH), F32),
            pltpu.VMEM((SUB, LRU_WIDTH), F32),
            pltpu.VMEM((SUB, LRU_WIDTH), F32),
            pltpu.VMEM((1, LRU_WIDTH), F32),
        ],
        compiler_params=pltpu.CompilerParams(
            dimension_semantics=("arbitrary", "arbitrary"), vmem_limit_bytes=VMEM_LIMIT),
        name="front",
    )(x, nw, w, cw, cb, wg, rgb, igb, lam, na)


def _split_bf16(x):
    hi = x.astype(BF16)
    lo = (x - hi.astype(F32)).astype(BF16)
    return hi, lo


def _pdot(x, y):
    (xh, xl), (yh, yl) = x, y
    lhs = jnp.concatenate([xh, xl, xh], axis=1)
    rhs = jnp.concatenate([yh, yh, yl], axis=0)
    return _dot(lhs, rhs)


N_LEVELS = 4
HEAD_GROUP = 8


def _inverse_masks():
    ri = lax.broadcasted_iota(jnp.int32, (CHUNK, CHUNK), 0)
    ci = lax.broadcasted_iota(jnp.int32, (CHUNK, CHUNK), 1)
    masks = [ri == ci, (ri // SUBLANES) == (ci // SUBLANES)]
    m = SUBLANES
    while m < CHUNK:
        masks.append(((ri // (2 * m)) == (ci // (2 * m))) & ((ri // m) != (ci // m)) & (ri > ci))
        m *= 2
    return jnp.stack(masks, axis=0).astype(BF16)


def _unit_lower_inverses(a_list, mask_ref, live_rows):
    eye = mask_ref[0]
    blk = mask_ref[1]
    a_s = [_split_bf16(a) for a in a_list]
    a0_s = [(ah * blk, al * blk) for ah, al in a_s]
    p2_s = [_split_bf16(_pdot(a0, a0)) for a0 in a0_s]
    p4_s = [_split_bf16(_pdot(p2, p2)) for p2 in p2_s]
    t1_s = [_split_bf16(_pdot((eye - a0h, -a0l), (eye + p2h, p2l))) for (a0h, a0l), (p2h, p2l) in zip(a0_s, p2_s)]
    t_list = [_pdot(t1, (eye + p4h, p4l)) for t1, (p4h, p4l) in zip(t1_s, p4_s)]
    for lvl in range(N_LEVELS):
        if SUBLANES << lvl >= live_rows:
            break
        sub = mask_ref[2 + lvl]
        t_s = [_split_bf16(t) for t in t_list]
        et_s = [_split_bf16(_pdot((ah * sub, al * sub), ts)) for (ah, al), ts in zip(a_s, t_s)]
        t_list = [t - _pdot(ts, et) for t, ts, et in zip(t_list, t_s, et_s)]
    return t_list


def _pad_rows(x, rows):
    if x.shape[0] == rows:
        return x
    return jnp.concatenate([x, jnp.zeros((rows - x.shape[0], x.shape[1]), x.dtype)], axis=0)


def _gdn_recurrence(lhs1_ref, lhs2_ref, u_ref, ge_ref, z_ref, nb_ref, yb_ref, s_ref, tr):
    for h in range(GDN_HEADS):
        s = s_ref[h]
        m1 = _dot(lhs1_ref[h], s.astype(BF16))
        w = u_ref[h] - m1[:CHUNK]
        m2 = _dot(lhs2_ref[h], w.astype(BF16))
        o = (m1[CHUNK:] + m2[:CHUNK])[:tr]
        s_ref[h] = ge_ref[h:h + 1, :] * s + m2[CHUNK:]
        sl = slice(h * GDN_DV, (h + 1) * GDN_DV)
        yb_ref[:, sl] = (_rms(o, nb_ref[...]) * _silu(z_ref[:, sl])).astype(BF16)


def _gdn_kernel(q_ref, k_ref, v_ref, tail_ref, z_ref, halo_ref, s0_ref, cw_ref, arow_ref, dtrow_ref, mask_ref,
                nb_ref, yb_ref, sout_ref, xe_ref, s_ref, lhs1_ref, lhs2_ref, u_ref, ge_ref,
                *, tr, nc, valid, overlap):
    c = pl.program_id(1)
    srcs = (q_ref, k_ref, v_ref)

    @pl.when(c == 0)
    def _():
        for i in range(3):
            xe_ref[i, 0:SUBLANES, :] = halo_ref[0, :, i * GDN_QK:(i + 1) * GDN_QK]
        s_ref[...] = s0_ref[0]
        lhs1_ref[...] = jnp.zeros(lhs1_ref.shape, BF16)
        lhs2_ref[...] = jnp.zeros(lhs2_ref.shape, BF16)
        u_ref[...] = jnp.zeros(u_ref.shape, F32)
        ge_ref[...] = jnp.ones(ge_ref.shape, F32)

    @pl.when(c > 0)
    def _():
        for i in range(3):
            xe_ref[i, 0:SUBLANES, :] = xe_ref[i, tr:tr + SUBLANES, :]

    recurrence = functools.partial(_gdn_recurrence, lhs1_ref, lhs2_ref, u_ref, ge_ref, z_ref, nb_ref, yb_ref,
                                   s_ref, tr)
    prepare = functools.partial(_gdn_prepare, srcs, tail_ref, cw_ref, arow_ref, dtrow_ref, mask_ref,
                                lhs1_ref, lhs2_ref, u_ref, ge_ref, xe_ref, tr, valid - jnp.minimum(c, nc - 1) * tr)
    if overlap:
        recurrence()
        prepare()
    else:
        pl.when(c > 0)(recurrence)
        pl.when(c < nc)(prepare)

    @pl.when(c == nc)
    def _():
        sout_ref[0] = s_ref[...]


def _gdn_prepare(srcs, tail_ref, cw_ref, arow_ref, dtrow_ref, mask_ref, lhs1_ref, lhs2_ref, u_ref, ge_ref, xe_ref,
                 tr, rows_left):
    qkv = []
    for i in range(3):
        xe_ref[i, SUBLANES:tr + SUBLANES, :] = srcs[i][...]
        y = _conv_from_scratch(xe_ref.at[i], cw_ref[:, i * GDN_QK:(i + 1) * GDN_QK], tr)
        qkv.append(_pad_rows(_silu(y), CHUNK))
    q_all, k_all, v_all = qkv

    tail = tail_ref[...]
    live =(lax.broadcasted_iota(jnp.int32, (tr, LANES), 0) < rows_left).astype(F32)
    beta = _pad_rows(jax.nn.sigmoid(tail) * live, CHUNK)
    g = _pad_rows(-jnp.exp(arow_ref[...]) * _softplus(tail + dtrow_ref[...]) * live, CHUNK)

    ri = lax.broadcasted_iota(jnp.int32, (CHUNK, CHUNK), 0)
    ci = lax.broadcasted_iota(jnp.int32, (CHUNK, CHUNK), 1)
    incl = ri >= ci
    strict = ri > ci
    gc = jnp.dot(incl.astype(F32), g, precision=lax.Precision.HIGHEST, preferred_element_type=F32)
    gc_t = gc.T
    ge_ref[...] = jnp.broadcast_to(
        jnp.exp(gc_t[LANE_G:LANE_G + GDN_HEADS, CHUNK - 1:CHUNK]), (GDN_HEADS, LANES))

    for h0 in range(0, GDN_HEADS, HEAD_GROUP):
        _gdn_prepare_heads(range(h0, h0 + HEAD_GROUP), q_all, k_all, v_all, gc, gc_t, beta, incl, strict,
                           mask_ref, lhs1_ref, lhs2_ref, u_ref, tr)


def _gdn_prepare_heads(heads, q_all, k_all, v_all, gc, gc_t, beta, incl, strict, mask_ref, lhs1_ref, lhs2_ref, u_ref,
                       live_rows):
    a_list, rhs_list = [], []
    for h in heads:
        sl = slice(h * GDN_DK, (h + 1) * GDN_DK)
        qh, kh, vh = q_all[:, sl], k_all[:, sl], v_all[:, sl]
        qh = qh * lax.rsqrt(jnp.sum(qh * qh, axis=-1, keepdims=True) + EPS) * (GDN_DK ** -0.5)
        kh = kh * lax.rsqrt(jnp.sum(kh * kh, axis=-1, keepdims=True) + EPS)
        gcol = gc[:, LANE_G + h:LANE_G + h + 1]
        grow = gc_t[LANE_G + h:LANE_G + h + 1, :]
        bcol = beta[:, LANE_BETA + h:LANE_BETA + h + 1]
        decay = jnp.where(incl, jnp.exp(jnp.where(incl, gcol - grow, 0.0)), 0.0)
        kb = kh.astype(BF16)
        qkk = lax.dot_general(jnp.concatenate([qh.astype(BF16), kb], axis=0), kb,
                              (((1,), (1,)), ((), ())), preferred_element_type=F32)
        a_list.append(jnp.where(strict, bcol * decay * qkk[CHUNK:], 0.0))
        egc = jnp.exp(gcol)
        rhs_list.append(jnp.concatenate([bcol * vh, (bcol * egc) * kh], axis=1))
        lhs1_ref[h, CHUNK:, :] = (qh * egc).astype(BF16)
        k_end = kh * jnp.exp(grow[:, CHUNK - 1:CHUNK] - gcol)
        lhs2_ref[h, 0:CHUNK, :] = (qkk[:CHUNK] * decay).astype(BF16)
        lhs2_ref[h, CHUNK:, :] = k_end.T.astype(BF16)

    t_list = _unit_lower_inverses(a_list, mask_ref, live_rows)
    for i, h in enumerate(heads):
        sol = _pdot(_split_bf16(t_list[i]), _split_bf16(rhs_list[i]))
        u_ref[h] = sol[:, :GDN_DV]
        lhs1_ref[h, 0:CHUNK, :] = sol[:, GDN_DV:].astype(BF16)


def _gdn(proj, halo, s0, cw, arow, dtrow, nbw, *, nb, tr, nc, row_block0, valid, overlap):
    prep_map = lambda col: (lambda b, c: (row_block0 + b * nc + jnp.minimum(c, nc - 1), col))
    const2 = lambda b, c: (0, 0)
    return pl.pallas_call(
        functools.partial(_gdn_kernel, tr=tr, nc=nc, valid=valid, overlap=overlap),
        grid=(nb, nc + 1),
        in_specs=[
            pl.BlockSpec((tr, GDN_QK), prep_map(COL_Q)),
            pl.BlockSpec((tr, GDN_QK), prep_map(COL_K)),
            pl.BlockSpec((tr, GDN_VW), prep_map(COL_V)),
            pl.BlockSpec((tr, LANES), prep_map(COL_TAIL)),
            pl.BlockSpec((tr, GDN_VW), lambda b, c: (row_block0 + b * nc + jnp.maximum(c - 1, 0), COL_Z)),
            pl.BlockSpec((1, SUBLANES, GDN_QKV), lambda b, c: (b, 0, 0)),
            pl.BlockSpec((1, GDN_HEADS, GDN_DK, GDN_DV), lambda b, c: (b, 0, 0, 0)),
            pl.BlockSpec((SUBLANES, GDN_QKV), const2),
            pl.BlockSpec((1, LANES), const2),
            pl.BlockSpec((1, LANES), const2),
            pl.BlockSpec((2 + N_LEVELS, CHUNK, CHUNK), lambda b, c: (0, 0, 0)),
            pl.BlockSpec((1, GDN_DV), const2),
        ],
        out_specs=[
            pl.BlockSpec((tr, GDN_VW), lambda b, c: (b * nc + jnp.maximum(c - 1, 0), 0)),
            pl.BlockSpec((1, GDN_HEADS, GDN_DK, GDN_DV), lambda b, c: (b, 0, 0, 0)),
        ],
        out_shape=[
            jax.ShapeDtypeStruct((nb * nc * tr, GDN_VW), BF16),
            jax.ShapeDtypeStruct((nb, GDN_HEADS, GDN_DK, GDN_DV), F32),
        ],
        scratch_shapes=[
            pltpu.VMEM((3, tr + SUBLANES, GDN_QK), F32),
            pltpu.VMEM((GDN_HEADS, GDN_DK, GDN_DV), F32),
            pltpu.VMEM((GDN_HEADS, 2 * CHUNK, GDN_DK), BF16),
            pltpu.VMEM((GDN_HEADS, 2 * CHUNK, GDN_DK), BF16),
            pltpu.VMEM((GDN_HEADS, CHUNK, GDN_DV), F32),
            pltpu.VMEM((GDN_HEADS, LANES), F32),
        ],
        compiler_params=pltpu.CompilerParams(
            dimension_semantics=("parallel", "arbitrary"), vmem_limit_bytes=VMEM_LIMIT),
        name="gdn",
    )(proj, proj, proj, proj, proj, halo, s0, cw, arow, dtrow, _inverse_masks(), nbw)


def _halo(hist):
    return jnp.pad(hist, ((0, 0), (SUBLANES - (CONV_W - 1), 0), (0, 0)))


def _taps(w):
    return jnp.pad(w, ((0, SUBLANES - CONV_W), (0, 0)))


def _gate_weights(rg_w, ig_w):
    eye = jnp.eye(4, dtype=F32)

    def bd(w):
        return jnp.einsum('qnij,nm->qnimj', w.reshape(4, 4, LRU_BLOCK, LRU_BLOCK), eye).reshape(
            4, 4 * LRU_BLOCK, 4 * LRU_BLOCK)

    return jnp.concatenate([bd(rg_w), bd(ig_w)], axis=2).astype(BF16)


def _lane_row(vals, lane0):
    return jnp.zeros((1, LANES), F32).at[0, lane0:lane0 + vals.shape[0]].set(vals)


def _mixer(x, l, st, p):
    state_conv_a, state_lru, state_conv_b, state_delta = st
    mix_nw = p['mix_norm'][l][None]
    cw_a, cb_a = _taps(p['conv_a_w'][l]), p['conv_a_b'][l][None]
    wg = _gate_weights(p['rg_w'][l], p['ig_w'][l])
    lru_args = (cw_a, cb_a, wg, p['rg_b'][l][None], p['ig_b'][l][None], p['lru_lambda'][l][None],
                p['norm_a'][l][None])
    cw_b = _taps(p['conv_b_w'][l])
    arow = _lane_row(p['a_log'][l], LANE_G)
    dtrow = _lane_row(p['dt_bias'][l], LANE_G)
    nbw = p['norm_b'][l][None]

    proj_p, ya_p, lru_p = _front(x, mix_nw, p['w_in'], l, *lru_args)
    ncp = TP // CHUNK
    yb_p, dl_p = _gdn(proj_p, jnp.zeros((BATCH, SUBLANES, GDN_QKV), F32),
                      jnp.zeros((BATCH, GDN_HEADS, GDN_DK, GDN_DV), F32), cw_b, arow, dtrow, nbw,
                      nb=BATCH, tr=CHUNK, nc=ncp, row_block0=0, valid=VALID_P, overlap=True)

    proj_s = _inproj(x, mix_nw, p['w_in'], l, tm=ROWS_S, row_block0=ROWS_P // ROWS_S, rows=ROWS_S)
    ya_s, lru_s = _lru(proj_s, _halo(state_conv_a[l]), state_lru[l][:, None, :], *lru_args,
                       nb=DEC_BATCH, tt=DEC_SEQ, nt=1, row_block0=0, last_tile=0, last_row=DEC_SEQ - 1)
    yb_s, dl_s = _gdn(proj_s, _halo(state_conv_b[l]), state_delta[l], cw_b, arow, dtrow, nbw,
                      nb=DEC_BATCH, tr=DEC_SEQ, nc=1, row_block0=0, valid=DEC_SEQ, overlap=False)

    x = _outproj(x, ya_p, yb_p, ya_s, yb_s, p['w_out'], l)

    pp = jnp.stack([proj_p[b * TP + VALID_P - 3:b * TP + VALID_P] for b in range(BATCH)], axis=0)
    ps = proj_s.reshape(DEC_BATCH, DEC_SEQ, N_IN_PAD)[:, DEC_SEQ - 3:]
    o2 = 2 * LRU_WIDTH
    new_p = (pp[..., :LRU_WIDTH], lru_p[:, 0], pp[..., o2:o2 + GDN_QKV], dl_p)
    new_s = (ps[..., :LRU_WIDTH], lru_s[:, 0], ps[..., o2:o2 + GDN_QKV], dl_s)
    return x, new_p, new_s


def kernel(x_prompt, x_sample, state_conv_a, state_lru, state_conv_b, state_delta, meta_tokens, ffn1_norm, ffn1_w_gate, ffn1_w_up, ffn1_w_down, mix_norm, w_in, conv_a_w, conv_a_b, rg_w, rg_b, ig_w, ig_b, lru_lambda, norm_a, conv_b_w, a_log, dt_bias, norm_b, w_out, ffn2_norm, ffn2_w_gate, ffn2_w_up, ffn2_w_down, final_norm):
    w_in_b = jnp.pad(w_in.astype(BF16), ((0, 0), (0, 0), (0, N_IN_PAD - N_IN)))
    p = dict(mix_norm=mix_norm, w_in=w_in_b, conv_a_w=conv_a_w, conv_a_b=conv_a_b, rg_w=rg_w, rg_b=rg_b,
             ig_w=ig_w, ig_b=ig_b, lru_lambda=lru_lambda, norm_a=norm_a, conv_b_w=conv_b_w, a_log=a_log,
             dt_bias=dt_bias, norm_b=norm_b, w_out=w_out.astype(BF16))
    ffn1 = (ffn1_w_gate, ffn1_w_up.astype(BF16), ffn1_w_down)
    ffn2 = (ffn2_w_gate, ffn2_w_up.astype(BF16), ffn2_w_down)
    pad = jnp.zeros((TP - VALID_P, D_MODEL), F32)
    parts = []
    for b in range(BATCH):
        parts += [meta_tokens, x_prompt[b], pad]
    x = jnp.concatenate(parts + [x_sample.reshape(ROWS_S, D_MODEL)], axis=0)
    st = (state_conv_a, state_lru, state_conv_b, state_delta)
    fw = final_norm[None]
    news_p, news_s = [], []
    for l in range(DEPTH):
        x = _ffn(x, ffn1_norm[l][None], *ffn1, fw, l, False)
        x, new_p, new_s = _mixer(x, l, st, p)
        x = _ffn(x, ffn2_norm[l][None], *ffn2, fw, l, l == DEPTH - 1)
        news_p.append(new_p)
        news_s.append(new_s)
    y_p = jnp.stack([x[b * TP + N_META:b * TP + VALID_P] for b in range(BATCH)], axis=0)
    y_s = x[ROWS_P:].reshape(DEC_BATCH, DEC_SEQ, D_MODEL)
    stack = lambda news, i: jnp.stack([n[i] for n in news], axis=0)
    return (y_p, y_s,
            stack(news_p, 0), stack(news_p, 1), stack(news_p, 2), stack(news_p, 3),
            stack(news_s, 0), stack(news_s, 1), stack(news_s, 2), stack(news_s, 3))
```
